```python
import math
import jax, jax.numpy as jnp
from jax import lax
import numpy as np

D_MODEL = 1024
BATCH = 8
SEQ = 2048
DEPTH = 4
DEC_BATCH = 32
DEC_SEQ = 8
PAST_LEN = 8192
PAGE_SIZE = 128

H_A = 8
HD_A = 64
PATTERNS = ((128, 1), (512, 4), (2048, 16))
G_A = len(PATTERNS)
W_MAX = max(w for w, _ in PATTERNS)
N_BUCKETS = 32
REL_MAX_DIST = W_MAX
P_B = 16
D_B = 512
G_B = D_B // P_B
N_B = 64
DT_MIN = 1e-3
DT_MAX = 1e-1
H_C = 4
DK_C = 64
DV_C = 128
FG_RANK = 16
GLA_TAU = 16.0
GLA_CHUNK = 64
D_FF = 4 * D_MODEL
EPS = 1e-6
NEG = -1e30

COLS = (G_A * H_A * HD_A, H_A * HD_A, H_A * HD_A, D_B, H_C * DK_C, H_C * DK_C, H_C * DV_C, H_C * DV_C,
        FG_RANK, D_MODEL, D_MODEL, D_MODEL)
IN_COLS = sum(COLS)

kernel_name = "hybrid_dilated_s5_gla_decoder_step"

F32 = jnp.float32


def _rmsnorm(x, g):
    xf = x.astype(F32)
    y = xf * lax.rsqrt(jnp.mean(xf * xf, axis=-1, keepdims=True) + EPS)
    return (y * g.astype(F32)).astype(x.dtype)


def _t5_bucket(dist):
    exact = N_BUCKETS // 2
    d = np.maximum(dist, 1).astype(np.float32)
    large = exact + (np.log(d / exact) / np.log(REL_MAX_DIST / exact) * (N_BUCKETS - exact)).astype(np.int32)
    large = np.minimum(large, N_BUCKETS - 1)
    return np.where(dist < exact, dist, large).astype(np.int32)


def _group_bias(rel_bias, g, dil, n_back):
    bk = _t5_bucket(np.arange(n_back + 1) * dil)
    return rel_bias[bk][:, g * H_A:(g + 1) * H_A].T.astype(F32)


def _dil_prompt(q, k, v, bias, dil, n_back):
    B, S, H, hd = q.shape
    L = S // dil
    nblk = -(-L // n_back)
    Lp = nblk * n_back

    def res(t):
        return t.reshape(B, L, dil, H, hd).transpose(0, 2, 1, 3, 4)

    qr = jnp.pad(res(q), ((0, 0), (0, 0), (0, Lp - L), (0, 0), (0, 0))).reshape(B, dil, nblk, n_back, H, hd)

    def kblocks(t):
        tp = jnp.pad(res(t), ((0, 0), (0, 0), (n_back, Lp - L), (0, 0), (0, 0)))
        tp = tp.reshape(B, dil, nblk + 1, n_back, H, hd)
        return jnp.concatenate([tp[:, :, :-1], tp[:, :, 1:]], axis=3)

    kb, vb = kblocks(k), kblocks(v)
    i = np.arange(n_back)[:, None]
    kk = np.arange(2 * n_back)[None, :]
    step = n_back + i - kk
    m_key = np.arange(nblk)[:, None, None] * n_back + kk[None] - n_back
    valid = ((step >= 0) & (step <= n_back))[None] & (m_key >= 0) & (m_key < L)
    s = jnp.einsum('brnqhd,brnkhd->brnhqk', qr, kb).astype(F32) + bias[:, np.clip(step, 0, n_back)]
    s = jnp.where(valid[None, None, :, None], s, NEG)
    mx = jnp.max(s, axis=-1, keepdims=True)
    p = jnp.exp(s - mx)
    den = jnp.sum(p, axis=-1, keepdims=True)
    o = jnp.einsum('brnhqk,brnkhd->brnqhd', p / den, vb.astype(F32))
    lse = jnp.swapaxes((mx + jnp.log(den))[..., 0], 3, 4)
    o = o.reshape(B, dil, Lp, H, hd)[:, :, :L].transpose(0, 2, 1, 3, 4).reshape(B, S, H, hd)
    lse = lse.reshape(B, dil, Lp, H)[:, :, :L].transpose(0, 2, 1, 3).reshape(B, S, H)
    return o, lse


def _dil_sample(q, keys, vals, bias, dil, n_back, wbuf):
    T = q.shape[1]
    idx = wbuf + np.arange(T)[:, None] - np.arange(n_back + 1)[None, :] * dil
    valid = idx >= 0
    idx = np.maximum(idx, 0)
    kg = keys[:, idx]
    vg = vals[:, idx]
    s = jnp.einsum('bthd,btkhd->bhtk', q, kg).astype(F32) + bias[:, None, :]
    s = jnp.where(valid, s, NEG)
    mx = jnp.max(s, axis=-1, keepdims=True)
    p = jnp.exp(s - mx)
    den = jnp.sum(p, axis=-1, keepdims=True)
    o = jnp.einsum('bhtk,btkhd->bthd', p / den, vg.astype(F32))
    lse = jnp.swapaxes((mx + jnp.log(den))[..., 0], 1, 2)
    return o, lse


def _cplx_combine(e1, e2):
    a1r, a1i, b1r, b1i = e1
    a2r, a2i, b2r, b2i = e2
    return (a2r * a1r - a2i * a1i, a2r * a1i + a2i * a1r,
            a2r * b1r - a2i * b1i + b2r, a2r * b1i + a2i * b1r + b2i)


def _s5(u, x0_re, x0_im, log_dt, a_re, a_im, b_re, b_im, c_re, c_im, d_skip):
    a_re, a_im, b_re, b_im, c_re, c_im = [t.astype(F32) for t in (a_re, a_im, b_re, b_im, c_re, c_im)]
    dt = jnp.exp(log_dt.astype(F32))[:, None]
    mag = jnp.exp(a_re * dt)
    ab_re = mag * jnp.cos(a_im * dt)
    ab_im = mag * jnp.sin(a_im * dt)
    den = a_re * a_re + a_im * a_im
    nr = ab_re - 1.0
    co_re = (nr * a_re + ab_im * a_im) / den
    co_im = (ab_im * a_re - nr * a_im) / den
    bb_re = co_re[..., None] * b_re - co_im[..., None] * b_im
    bb_im = co_re[..., None] * b_im + co_im[..., None] * b_re
    bu_re = jnp.einsum('btgp,gnp->btgn', u, bb_re)
    bu_im = jnp.einsum('btgp,gnp->btgn', u, bb_im)
    x0_re, x0_im = x0_re.astype(F32), x0_im.astype(F32)
    bu_re = bu_re.at[:, 0].add(ab_re * x0_re - ab_im * x0_im)
    bu_im = bu_im.at[:, 0].add(ab_re * x0_im + ab_im * x0_re)
    shp = bu_re.shape
    _, _, xr, xi = lax.associative_scan(
        _cplx_combine, (jnp.broadcast_to(ab_re, shp), jnp.broadcast_to(ab_im, shp), bu_re, bu_im), axis=1)
    y = (jnp.einsum('btgn,gpn->btgp', xr, c_re) - jnp.einsum('btgn,gpn->btgp', xi, c_im)
         + d_skip.astype(F32).reshape(G_B, P_B) * u)
    return y, xr[:, -1], xi[:, -1]


def _gla(q, k, v, log_a, s0):
    B, T, H, dk = q.shape
    dv = v.shape[-1]
    C = math.gcd(T, GLA_CHUNK)
    n = T // C

    def r(t):
        return t.reshape(B, n, C, H, t.shape[-1])

    q, k, v, log_a = r(q), r(k), r(v), r(log_a)
    b = jnp.cumsum(log_a, axis=2)
    b_last = b[:, :, -1]
    qt = q * jnp.exp(b)
    kt = k * jnp.exp(-b)
    ds = jnp.einsum('bnchk,bnchv->bnhkv', k * jnp.exp(b_last[:, :, None] - b), v)

    def step(s, inp):
        dec, d = inp
        return dec[..., None] * s + d, s

    s_fin, s_prev = lax.scan(step, s0.astype(F32), (jnp.moveaxis(jnp.exp(b_last), 1, 0), jnp.moveaxis(ds, 1, 0)))
    s_prev = jnp.moveaxis(s_prev, 0, 1)
    causal = np.tril(np.ones((C, C), dtype=bool))
    att = jnp.where(causal, jnp.einsum('bnchk,bnshk->bnhcs', qt, kt), 0.0)
    o = jnp.einsum('bnhcs,bnshv->bnchv', att, v) + jnp.einsum('bnchk,bnhkv->bnchv', qt, s_prev)
    return o.reshape(B, T, H, dv), s_fin


def _mixer(h, lw, rel_bias, kbuf, vbuf, ssm_re0, ssm_im0, gla0):
    (w_in, w_o_a, log_dt, a_re, a_im, b_re, b_im, c_re, c_im, d_skip, w_glu,
     w_fg2, b_fg, gla_norm, w_o_c, w_out) = lw
    B, T, _ = h.shape
    z = h @ w_in
    qa, ka, va, ub, qc, kc, vc, rc, fc, ga, gb, gc = jnp.split(z, np.cumsum(COLS)[:-1].tolist(), axis=-1)

    qa = qa.reshape(B, T, G_A, H_A, HD_A) * (HD_A ** -0.5)
    ka = ka.reshape(B, T, H_A, HD_A)
    va = va.reshape(B, T, H_A, HD_A)
    if kbuf is None:
        keys, vals = ka, va
        new_k, new_v = ka[:, -min(W_MAX, T):], va[:, -min(W_MAX, T):]
        wbuf = 0
    else:
        wbuf = kbuf.shape[1]
        keys = jnp.concatenate([kbuf.astype(ka.dtype), ka], axis=1)
        vals = jnp.concatenate([vbuf.astype(va.dtype), va], axis=1)
        new_k, new_v = keys[:, -wbuf:], vals[:, -wbuf:]
    outs, lses = [], []
    for g, (win, dil) in enumerate(PATTERNS):
        n_back = win // dil
        bias = _group_bias(rel_bias, g, dil, n_back)
        if kbuf is None:
            o, lse = _dil_prompt(qa[:, :, g], keys, vals, bias, dil, n_back)
        else:
            o, lse = _dil_sample(qa[:, :, g], keys, vals, bias, dil, n_back, wbuf)
        outs.append(o)
        lses.append(lse)
    wts = jax.nn.softmax(jnp.stack(lses), axis=0)
    o_a = jnp.einsum('gbth,gbthd->bthd', wts, jnp.stack(outs))
    y_a = o_a.reshape(B, T, H_A * HD_A).astype(h.dtype) @ w_o_a

    u = ub.reshape(B, T, G_B, P_B).astype(F32)
    y, ssm_re, ssm_im = _s5(u, ssm_re0, ssm_im0, log_dt, a_re, a_im, b_re, b_im, c_re, c_im, d_skip)
    y = jax.nn.gelu(y.reshape(B, T, D_B))
    ab = y @ w_glu.astype(F32)
    y_b = (ab[..., :D_MODEL] * jax.nn.sigmoid(ab[..., D_MODEL:])).astype(h.dtype)

    qc = qc.reshape(B, T, H_C, DK_C).astype(F32) * (DK_C ** -0.5)
    kc = kc.reshape(B, T, H_C, DK_C).astype(F32)
    vc = vc.reshape(B, T, H_C, DV_C).astype(F32)
    log_a = (jax.nn.log_sigmoid((fc @ w_fg2 + b_fg).astype(F32)) / GLA_TAU).reshape(B, T, H_C, DK_C)
    o_c, gla_s = _gla(qc, kc, vc, log_a, gla0)
    o_c = _rmsnorm(o_c, gla_norm)
    y_c = (o_c.reshape(B, T, H_C * DV_C) * jax.nn.silu(rc.astype(F32))).astype(h.dtype) @ w_o_c

    m = jax.nn.sigmoid(ga) * y_a + jax.nn.sigmoid(gb) * y_b + jax.nn.sigmoid(gc) * y_c
    return (m @ w_out).astype(h.dtype), (new_k, new_v, ssm_re, ssm_im, gla_s)


def _trunk(x, layer_w, norm_mix, norm_mlp, w_up, w_down, norm_final, rel_bias, caches):
    B = x.shape[0]
    new = [[] for _ in range(5)]
    for l in range(DEPTH):
        lw = tuple(w[l] for w in layer_w)
        if caches is None:
            c = (None, None, jnp.zeros((B, G_B, N_B), F32), jnp.zeros((B, G_B, N_B), F32),
                 jnp.zeros((B, H_C, DK_C, DV_C), F32))
        else:
            c = tuple(t[l] for t in caches)
        mix, st = _mixer(_rmsnorm(x, norm_mix[l]), lw, rel_bias, *c)
        x = x + mix
        hm = _rmsnorm(x, norm_mlp[l])
        x = x + (jnp.square(jax.nn.relu(hm @ w_up[l])) @ w_down[l]).astype(x.dtype)
        for lst, s in zip(new, st):
            lst.append(s)
    return _rmsnorm(x, norm_final), [jnp.stack(s) for s in new]


def setup_inputs(seed: int = 0) -> dict:
    key = jax.random.key(seed)
    ks = list(jax.random.split(key, 32))
    cnt = [0]

    def nk():
        cnt[0] += 1
        return ks[cnt[0] - 1]

    def nrm(shape, scale):
        return jax.random.normal(nk(), shape, F32) * scale

    D = D_MODEL
    wbuf = min(W_MAX, PAST_LEN)
    n_idx = jnp.arange(N_B, dtype=F32)
    out = {}
    out['x_prompt'] = nrm((BATCH, SEQ, D), 1.0)
    out['x_sample'] = nrm((DEC_BATCH, DEC_SEQ, D), 1.0)
    out['cache_k_win'] = nrm((DEPTH, DEC_BATCH, wbuf, H_A, HD_A), 1.0)
    out['cache_v_win'] = nrm((DEPTH, DEC_BATCH, wbuf, H_A, HD_A), 1.0)
    out['state_ssm_re'] = nrm((DEPTH, DEC_BATCH, G_B, N_B), 0.1)
    out['state_ssm_im'] = nrm((DEPTH, DEC_BATCH, G_B, N_B), 0.1)
    out['state_gla'] = nrm((DEPTH, DEC_BATCH, H_C, DK_C, DV_C), 1.0)
    out['rel_bias'] = nrm((N_BUCKETS, G_A * H_A), 0.5)
    out['norm_mix'] = 1.0 + nrm((DEPTH, D), 0.02)
    out['w_in'] = nrm((DEPTH, D, IN_COLS), D ** -0.5)
    out['w_o_a'] = nrm((DEPTH, H_A * HD_A, D), (H_A * HD_A) ** -0.5)
    out['s5_log_dt'] = math.log(DT_MIN) + jax.random.uniform(nk(), (DEPTH, G_B), F32) * (math.log(DT_MAX) - math.log(DT_MIN))
    out['s5_a_re'] = -0.5 + nrm((DEPTH, G_B, N_B), 0.01)
    out['s5_a_im'] = math.pi * n_idx + nrm((DEPTH, G_B, N_B), 0.01)
    out['s5_b_re'] = nrm((DEPTH, G_B, N_B, P_B), (2 * P_B) ** -0.5)
    out['s5_b_im'] = nrm((DEPTH, G_B, N_B, P_B), (2 * P_B) ** -0.5)
    out['s5_c_re'] = nrm((DEPTH, G_B, P_B, N_B), N_B ** -0.5)
    out['s5_c_im'] = nrm((DEPTH, G_B, P_B, N_B), N_B ** -0.5)
    out['s5_d'] = nrm((DEPTH, D_B), 1.0)
    out['w_glu'] = nrm((DEPTH, D_B, 2 * D), D_B ** -0.5)
    out['w_fg2'] = nrm((DEPTH, FG_RANK, H_C * DK_C), FG_RANK ** -0.5)
    out['b_fg'] = nrm((DEPTH, H_C * DK_C), 0.1)
    out['gla_norm'] = 1.0 + nrm((DEPTH, DV_C), 0.02)
    out['w_o_c'] = nrm((DEPTH, H_C * DV_C, D), (H_C * DV_C) ** -0.5)
    out['w_out'] = nrm((DEPTH, D, D), D ** -0.5)
    out['norm_mlp'] = 1.0 + nrm((DEPTH, D), 0.02)
    out['w_up'] = nrm((DEPTH, D, D_FF), D ** -0.5)
    out['w_down'] = nrm((DEPTH, D_FF, D), D_FF ** -0.5)
    out['norm_final'] = 1.0 + nrm((D,), 0.02)
    return out


def reference(x_prompt, x_sample, cache_k_win, cache_v_win, state_ssm_re, state_ssm_im, state_gla,
              rel_bias, norm_mix, w_in, w_o_a, s5_log_dt, s5_a_re, s5_a_im, s5_b_re, s5_b_im,
              s5_c_re, s5_c_im, s5_d, w_glu, w_fg2, b_fg, gla_norm, w_o_c, w_out,
              norm_mlp, w_up, w_down, norm_final):
    layer_w = (w_in, w_o_a, s5_log_dt, s5_a_re, s5_a_im, s5_b_re, s5_b_im, s5_c_re, s5_c_im, s5_d,
               w_glu, w_fg2, b_fg, gla_norm, w_o_c, w_out)
    y_prompt, st_p = _trunk(x_prompt, layer_w, norm_mix, norm_mlp, w_up, w_down, norm_final, rel_bias, None)
    y_sample, st_s = _trunk(x_sample, layer_w, norm_mix, norm_mlp, w_up, w_down, norm_final, rel_bias,
                            (cache_k_win, cache_v_win, state_ssm_re, state_ssm_im, state_gla))
    k_win_p, v_win_p, ssm_re_p, ssm_im_p, gla_p = st_p
    k_win_s, v_win_s, ssm_re_s, ssm_im_s, gla_s = st_s
    return (y_prompt, y_sample, k_win_p, v_win_p, k_win_s, v_win_s,
            ssm_re_p, ssm_im_p, ssm_re_s, ssm_im_s, gla_p, gla_s)
```

```python
import functools
import math

import numpy as np
import jax
import jax.numpy as jnp
from jax import lax
from jax.experimental import pallas as pl
from jax.experimental.pallas import tpu as pltpu

F32 = jnp.float32
BF16 = jnp.bfloat16

D_MODEL = 1024
DEPTH = 4
H_A = 8
HD_A = 64
D_A = H_A * HD_A
PATTERNS = ((128, 1), (512, 4), (2048, 16))
G_A = len(PATTERNS)
N_BACK = 128
W_MAX = 2048
N_BUCKETS = 32
P_B = 16
D_B = 512
G_B = D_B // P_B
N_B = 64
D_STATE = G_B * N_B
H_C = 4
DK_C = 64
DV_C = 128
FG_RANK = 16
FG_PAD = 128
GLA_TAU = 16.0
GLA_CHUNK = 64
D_FF = 4 * D_MODEL
EPS = 1e-6
NEG = -1e30

SEG_WIDTHS = (G_A * D_A, D_A, D_A, D_B, H_C * DK_C, H_C * DK_C, H_C * DV_C, H_C * DV_C, FG_PAD, 3 * D_MODEL)
SEG_OFFS = tuple(int(v) for v in np.cumsum((0,) + SEG_WIDTHS))
IN_COLS_PAD = SEG_OFFS[-1]

V7X_VMEM_BYTES = 64 * 1024 * 1024
VMEM_LIMIT = 56 * 1024 * 1024


def _cparams(sem):
    return pltpu.CompilerParams(dimension_semantics=sem, vmem_limit_bytes=VMEM_LIMIT)


def _resident(shape):
    nd = len(shape)
    return pl.BlockSpec(shape, lambda *_: (0,) * nd, pipeline_mode=pl.Buffered(1))


def _rms(x, g):
    return x * lax.rsqrt(jnp.mean(x * x, axis=-1, keepdims=True) + EPS) * g


def _dot(a, b):
    return jnp.dot(a, b, preferred_element_type=F32)


def _dot_nt(a, b):
    return lax.dot_general(a, b, (((1,), (1,)), ((), ())), preferred_element_type=F32)


def _dot_tn(a, b):
    return lax.dot_general(a, b, (((0,), (0,)), ((), ())), preferred_element_type=F32)


def _in_proj_kernel(x_ref, g_ref, w_ref, *out_refs):
    hb = _rms(x_ref[...], g_ref[...]).astype(BF16)
    for s, o_ref in enumerate(out_refs):
        z = _dot(hb, w_ref[:, SEG_OFFS[s]:SEG_OFFS[s + 1]])
        if s in (0, 4):
            z = z * 0.125
        o_ref[...] = z


def _in_proj(x, g, w, tm):
    m = x.shape[0]
    row = lambda i: (i, 0)
    return pl.pallas_call(
        _in_proj_kernel,
        grid=(m // tm,),
        in_specs=[pl.BlockSpec((tm, D_MODEL), row), _resident((1, D_MODEL)), _resident((D_MODEL, IN_COLS_PAD))],
        out_specs=[pl.BlockSpec((tm, wd), row) for wd in SEG_WIDTHS],
        out_shape=[jax.ShapeDtypeStruct((m, wd), F32) for wd in SEG_WIDTHS],
        compiler_params=_cparams(("parallel",)),
        name="in_proj",
    )(x, g, w)


def _attn_prompt_kernel(q_ref, kp_ref, kc_ref, vp_ref, vc_ref, bias_ref, o_ref, lse_ref):
    mb = pl.program_id(2)
    shp = (N_BACK, 2 * N_BACK)
    qi = lax.broadcasted_iota(jnp.int32, shp, 0)
    kk = lax.broadcasted_iota(jnp.int32, shp, 1)
    step = N_BACK + qi - kk
    valid = (step >= 0) & (step <= N_BACK) & ((kk >= N_BACK) | (mb > 0))
    q = q_ref[0].astype(BF16)
    k = jnp.concatenate([kp_ref[0], kc_ref[0]], axis=0).astype(BF16)
    v = jnp.concatenate([vp_ref[0], vc_ref[0]], axis=0).astype(BF16)
    for h in range(H_A):
        sl = slice(h * HD_A, (h + 1) * HD_A)
        s = jnp.where(valid, _dot_nt(q[:, sl], k[:, sl]) + bias_ref[h], NEG)
        mx = jnp.max(s, axis=-1, keepdims=True)
        p = jnp.exp(s - mx)
        den = jnp.sum(p, axis=-1, keepdims=True)
        o_ref[0, :, sl] = _dot(p.astype(BF16), v[:, sl]) / den
        lse_ref[0, :, sl] = jnp.broadcast_to(mx + jnp.log(den), (N_BACK, HD_A))


def _attn_prompt(qa, ka, va, biasmat, g, dil, batch, seq):
    L = seq // dil
    nblk = L // N_BACK
    qv = qa.reshape(batch, L, dil * G_A * D_A)
    kv = ka.reshape(batch, L, dil * D_A)
    vv = va.reshape(batch, L, dil * D_A)
    blk = (1, N_BACK, D_A)
    cur = lambda b, r, i: (b, i, r)
    prev = lambda b, r, i: (b, jnp.maximum(i - 1, 0), r)
    o, lse = pl.pallas_call(
        _attn_prompt_kernel,
        grid=(batch, dil, nblk),
        in_specs=[pl.BlockSpec(blk, lambda b, r, i: (b, i, r * G_A + g)),
                  pl.BlockSpec(blk, prev), pl.BlockSpec(blk, cur),
                  pl.BlockSpec(blk, prev), pl.BlockSpec(blk, cur),
                  _resident((H_A, N_BACK, 2 * N_BACK))],
        out_specs=[pl.BlockSpec(blk, cur), pl.BlockSpec(blk, cur)],
        out_shape=[jax.ShapeDtypeStruct((batch, L, dil * D_A), F32)] * 2,
        compiler_params=_cparams(("parallel", "parallel", "parallel")),
        name=f"attn_prompt_g{g}",
    )(qv, kv, kv, vv, vv, biasmat)
    return o.reshape(batch * seq, D_A), lse.reshape(batch * seq, D_A)


def _attn_sample_kernel(q_ref, kc_ref, vc_ref, kn_ref, vn_ref, bc_ref, bn_ref, o_ref, nk_ref, nv_ref, *, wbuf, t_new):
    rows = G_A * t_new
    nk_ref[0, 0:wbuf - t_new, :] = kc_ref[0, 0, t_new:wbuf, :]
    nk_ref[0, wbuf - t_new:wbuf, :] = kn_ref[0]
    nv_ref[0, 0:wbuf - t_new, :] = vc_ref[0, 0, t_new:wbuf, :]
    nv_ref[0, wbuf - t_new:wbuf, :] = vn_ref[0]

    def allowed(shape, key0):
        rr = lax.broadcasted_iota(jnp.int32, shape, 0)
        col = lax.broadcasted_iota(jnp.int32, shape, 1)
        grp = rr >> (t_new.bit_length() - 1)
        dist = wbuf + (rr - grp * t_new) - (col + key0)
        dm1 = jnp.where(grp == 0, PATTERNS[0][1] - 1, jnp.where(grp == 1, PATTERNS[1][1] - 1, PATTERNS[2][1] - 1))
        win = jnp.where(grp == 0, PATTERNS[0][0], jnp.where(grp == 1, PATTERNS[1][0], PATTERNS[2][0]))
        return (dist >= 0) & ((dist & dm1) == 0) & (dist <= win)

    ok_c = allowed((rows, wbuf), 0)
    ok_n = allowed((rows, t_new), wbuf)
    q = q_ref[0].astype(BF16)
    kc = kc_ref[0, 0].astype(BF16)
    vc = vc_ref[0, 0].astype(BF16)
    kn = kn_ref[0].astype(BF16)
    vn = vn_ref[0].astype(BF16)
    for h in range(H_A):
        sl = slice(h * HD_A, (h + 1) * HD_A)
        qh = q[:, sl]
        sc = jnp.where(ok_c, _dot_nt(qh, kc[:, sl]) + bc_ref[h], NEG)
        sn = jnp.where(ok_n, _dot_nt(qh, kn[:, sl]) + bn_ref[h], NEG)
        mx = jnp.maximum(jnp.max(sc, axis=-1, keepdims=True), jnp.max(sn, axis=-1, keepdims=True))
        mt = jnp.maximum(jnp.maximum(mx[0:t_new], mx[t_new:2 * t_new]), mx[2 * t_new:3 * t_new])
        mall = jnp.concatenate([mt, mt, mt], axis=0)
        pc = jnp.exp(sc - mall)
        pn = jnp.exp(sn - mall)
        den = jnp.sum(pc, axis=-1, keepdims=True) + jnp.sum(pn, axis=-1, keepdims=True)
        acc = _dot(pc.astype(BF16), vc[:, sl]) + _dot(pn.astype(BF16), vn[:, sl])
        dt = den[0:t_new] + den[t_new:2 * t_new] + den[2 * t_new:3 * t_new]
        at = acc[0:t_new] + acc[t_new:2 * t_new] + acc[2 * t_new:3 * t_new]
        o_ref[0, :, sl] = at / dt


def _attn_sample(q, kcache, vcache, layer, kn, vn, bias_c, bias_n):
    batch, wbuf = kcache.shape[1], kcache.shape[2]
    t_new = kn.shape[1]
    rows = G_A * t_new
    b3 = lambda b: (b, 0, 0)
    cache_spec = pl.BlockSpec((1, 1, wbuf, D_A), lambda b: (layer, b, 0, 0))
    return pl.pallas_call(
        functools.partial(_attn_sample_kernel, wbuf=wbuf, t_new=t_new),
        grid=(batch,),
        in_specs=[pl.BlockSpec((1, rows, D_A), b3), cache_spec, cache_spec,
                  pl.BlockSpec((1, t_new, D_A), b3), pl.BlockSpec((1, t_new, D_A), b3),
                  _resident((H_A, rows, wbuf)), _resident((H_A, rows, t_new))],
        out_specs=[pl.BlockSpec((1, t_new, D_A), b3), pl.BlockSpec((1, wbuf, D_A), b3),
                   pl.BlockSpec((1, wbuf, D_A), b3)],
        out_shape=[jax.ShapeDtypeStruct((batch, t_new, D_A), F32),
                   jax.ShapeDtypeStruct((batch, wbuf, D_A), F32),
                   jax.ShapeDtypeStruct((batch, wbuf, D_A), F32)],
        compiler_params=_cparams(("parallel",)),
        name="attn_sample",
    )(q, kcache, vcache, kn, vn, bias_c, bias_n)


S5_KB = 128
S5_NBLK = D_B // S5_KB
S5_SB = D_STATE // S5_NBLK


def _s5_kernel(u_ref, x0r_ref, x0i_ref, ar_ref, ai_ref, bre_ref, bim_ref, cre_ref, cim_ref, d_ref, wglu_ref,
               yb_ref, xr_out, xi_out, bur, bui, sr, si, *, nb, tc):
    @pl.when(pl.program_id(0) == 0)
    def _():
        sr[...] = x0r_ref[...]
        si[...] = x0i_ref[...]

    u = u_ref[...]
    ub = u.astype(BF16)
    for j in range(S5_NBLK):
        uj = ub[:, j * S5_KB:(j + 1) * S5_KB]
        bur[:, j * S5_SB:(j + 1) * S5_SB] = _dot(uj, bre_ref[j])
        bui[:, j * S5_SB:(j + 1) * S5_SB] = _dot(uj, bim_ref[j])

    ar = ar_ref[...]
    ai = ai_ref[...]

    def step(t, carry):
        xr, xi = carry
        rows = pl.ds(pl.multiple_of(t * nb, nb), nb)
        nr = ar * xr - ai * xi + bur[rows, :]
        ni = ar * xi + ai * xr + bui[rows, :]
        bur[rows, :] = nr
        bui[rows, :] = ni
        return nr, ni

    xr, xi = lax.fori_loop(0, tc, step, (sr[...], si[...]))
    sr[...] = xr
    si[...] = xi
    xr_out[...] = xr
    xi_out[...] = xi

    ys = []
    for j in range(S5_NBLK):
        cs = slice(j * S5_SB, (j + 1) * S5_SB)
        ys.append(_dot(bur[:, cs].astype(BF16), cre_ref[j]) - _dot(bui[:, cs].astype(BF16), cim_ref[j]))
    y = jnp.concatenate(ys, axis=-1) + d_ref[...] * u
    ab = _dot(jax.nn.gelu(y).astype(BF16), wglu_ref[...])
    yb_ref[...] = ab[:, :D_MODEL] * jax.nn.sigmoid(ab[:, D_MODEL:])


def _s5(u_tb, x0r, x0i, prm, wglu, nb, tc):
    rows = u_tb.shape[0]
    steps = rows // (tc * nb)
    r = tc * nb
    row = lambda i: (i, 0)
    st = pl.BlockSpec((nb, D_STATE), lambda i: (0, 0))
    return pl.pallas_call(
        functools.partial(_s5_kernel, nb=nb, tc=tc),
        grid=(steps,),
        in_specs=[pl.BlockSpec((r, D_B), row), st, st,
                  _resident((1, D_STATE)), _resident((1, D_STATE)),
                  _resident((S5_NBLK, S5_KB, S5_SB)), _resident((S5_NBLK, S5_KB, S5_SB)),
                  _resident((S5_NBLK, S5_SB, S5_KB)), _resident((S5_NBLK, S5_SB, S5_KB)),
                  _resident((1, D_B)), _resident((D_B, 2 * D_MODEL))],
        out_specs=[pl.BlockSpec((r, D_MODEL), row), st, st],
        out_shape=[jax.ShapeDtypeStruct((rows, D_MODEL), F32),
                   jax.ShapeDtypeStruct((nb, D_STATE), F32), jax.ShapeDtypeStruct((nb, D_STATE), F32)],
        scratch_shapes=[pltpu.VMEM((r, D_STATE), F32), pltpu.VMEM((r, D_STATE), F32),
                        pltpu.VMEM((nb, D_STATE), F32), pltpu.VMEM((nb, D_STATE), F32)],
        compiler_params=_cparams(("arbitrary",)),
        name="s5",
    )(u_tb, x0r, x0i, prm["ar"], prm["ai"], prm["bre"], prm["bim"], prm["cre"], prm["cim"], prm["d"], wglu)


def _s5_params(log_dt, a_re, a_im, b_re, b_im, c_re, c_im, d_skip):
    dt = jnp.exp(log_dt)[:, None]
    mag = jnp.exp(a_re * dt)
    ab_re = mag * jnp.cos(a_im * dt)
    ab_im = mag * jnp.sin(a_im * dt)
    den = a_re * a_re + a_im * a_im
    nr = ab_re - 1.0
    co_re = (nr * a_re + ab_im * a_im) / den
    co_im = (ab_im * a_re - nr * a_im) / den
    bb_re = co_re[..., None] * b_re - co_im[..., None] * b_im
    bb_im = co_re[..., None] * b_im + co_im[..., None] * b_re
    gpb = G_B // S5_NBLK
    eye = jnp.eye(gpb, dtype=F32)

    def pack_b(bb):
        t = bb.reshape(S5_NBLK, gpb, N_B, P_B)
        m = jnp.einsum('jgnp,gh->jgphn', t, eye)
        return m.reshape(S5_NBLK, gpb * P_B, gpb * N_B).astype(BF16)

    def pack_c(cc):
        t = cc.reshape(S5_NBLK, gpb, P_B, N_B)
        m = jnp.einsum('jgpn,gh->jgnhp', t, eye)
        return m.reshape(S5_NBLK, gpb * N_B, gpb * P_B).astype(BF16)

    return dict(ar=ab_re.reshape(1, D_STATE), ai=ab_im.reshape(1, D_STATE),
                bre=pack_b(bb_re), bim=pack_b(bb_im), cre=pack_c(c_re), cim=pack_c(c_im),
                d=d_skip.reshape(1, D_B))


def _gla_kernel(q_ref, k_ref, v_ref, r_ref, fc_ref, wfg_ref, bfg_ref, gn_ref, s0_ref, y_ref, sfin_ref, st,
                *, chunk, nchunk):
    tb = pl.program_id(1)

    @pl.when(tb == 0)
    def _():
        for h in range(H_C):
            st[h] = s0_ref[0, h].T

    rows_blk = chunk * nchunk
    la = jax.nn.log_sigmoid(_dot(fc_ref[...].astype(BF16), wfg_ref[...]) + bfg_ref[...]) / GLA_TAU
    ri = lax.broadcasted_iota(jnp.int32, (rows_blk, rows_blk), 0)
    ci = lax.broadcasted_iota(jnp.int32, (rows_blk, rows_blk), 1)
    sh = chunk.bit_length() - 1
    same = (ri >> sh) == (ci >> sh)
    tri = jnp.where(same & (ci <= ri), 1.0, 0.0).astype(BF16)
    la_hi = la.astype(BF16)
    la_lo = (la - la_hi.astype(F32)).astype(BF16)
    bcum = _dot(tri, la_hi) + _dot(tri, la_lo)
    causal = lax.broadcasted_iota(jnp.int32, (chunk, chunk), 1) <= lax.broadcasted_iota(jnp.int32, (chunk, chunk), 0)
    gn = gn_ref[...]
    for c in range(nchunk):
        rs = slice(c * chunk, (c + 1) * chunk)
        b = bcum[rs]
        bl = b[chunk - 1:chunk]
        qt = (q_ref[rs, :] * jnp.exp(b)).astype(BF16)
        kt = (k_ref[rs, :] * jnp.exp(-b)).astype(BF16)
        kd = (k_ref[rs, :] * jnp.exp(bl - b)).astype(BF16)
        dec = jnp.exp(bl)
        vb = v_ref[rs, :].astype(BF16)
        for h in range(H_C):
            ks = slice(h * DK_C, (h + 1) * DK_C)
            vs = slice(h * DV_C, (h + 1) * DV_C)
            s_t = st[h]
            att = jnp.where(causal, _dot_nt(qt[:, ks], kt[:, ks]), 0.0)
            o = _dot(att.astype(BF16), vb[:, vs]) + _dot_nt(qt[:, ks], s_t.astype(BF16))
            st[h] = s_t * dec[:, ks] + _dot_tn(vb[:, vs], kd[:, ks])
            y_ref[rs, vs] = _rms(o, gn) * jax.nn.silu(r_ref[rs, vs])

    @pl.when(tb == pl.num_programs(1) - 1)
    def _():
        for h in range(H_C):
            sfin_ref[0, h] = st[h].T


def _gla(qc, kc, vc, rc, fc, wfg, bfg, gnorm, s0, batch, seq):
    chunk = math.gcd(seq, GLA_CHUNK)
    nchunk = min(seq // chunk, 4)
    rows_blk = chunk * nchunk
    nt = seq // rows_blk
    row = lambda b, t: (b * nt + t, 0)
    st_spec = pl.BlockSpec((1, H_C, DK_C, DV_C), lambda b, t: (b, 0, 0, 0))
    return pl.pallas_call(
        functools.partial(_gla_kernel, chunk=chunk, nchunk=nchunk),
        grid=(batch, nt),
        in_specs=[pl.BlockSpec((rows_blk, H_C * DK_C), row), pl.BlockSpec((rows_blk, H_C * DK_C), row),
                  pl.BlockSpec((rows_blk, H_C * DV_C), row), pl.BlockSpec((rows_blk, H_C * DV_C), row),
                  pl.BlockSpec((rows_blk, FG_PAD), row),
                  _resident((FG_PAD, H_C * DK_C)), _resident((1, H_C * DK_C)), _resident((1, DV_C)), st_spec],
        out_specs=[pl.BlockSpec((rows_blk, H_C * DV_C), row), st_spec],
        out_shape=[jax.ShapeDtypeStruct((batch * seq, H_C * DV_C), F32),
                   jax.ShapeDtypeStruct((batch, H_C, DK_C, DV_C), F32)],
        scratch_shapes=[pltpu.VMEM((H_C, DV_C, DK_C), F32)],
        compiler_params=_cparams(("parallel", "arbitrary")),
        name="gla",
    )(qc, kc, vc, rc, fc, wfg, bfg, gnorm, s0)


def _post_kernel(*refs, n_groups, final):
    x_ref = refs[0]
    att_refs = refs[1:1 + 2 * n_groups] if n_groups > 1 else refs[1:2]
    rest = refs[len(att_refs) + 1:]
    (gates_ref, yb_ref, yc_ref, woa_ref, woc_ref, wout_ref, gmlp_ref, wup_ref, wdn_ref, gfin_ref, out_ref) = rest
    if n_groups > 1:
        lses = [att_refs[2 * g + 1][...] for g in range(n_groups)]
        mx = functools.reduce(jnp.maximum, lses)
        ws = [jnp.exp(l - mx) for l in lses]
        den = functools.reduce(lambda a, b: a + b, ws)
        o_a = functools.reduce(lambda a, b: a + b, [w * att_refs[2 * g][...] for g, w in enumerate(ws)]) / den
    else:
        o_a = att_refs[0][...]
    y_a = _dot(o_a.astype(BF16), woa_ref[...])
    y_c = _dot(yc_ref[...].astype(BF16), woc_ref[...])
    gates = gates_ref[...]
    m = (jax.nn.sigmoid(gates[:, :D_MODEL]) * y_a + jax.nn.sigmoid(gates[:, D_MODEL:2 * D_MODEL]) * yb_ref[...]
         + jax.nn.sigmoid(gates[:, 2 * D_MODEL:]) * y_c)
    x1 = x_ref[...] + _dot(m.astype(BF16), wout_ref[...])
    hm = _rms(x1, gmlp_ref[...]).astype(BF16)
    up = _dot(hm, wup_ref[...])
    act = jnp.square(jnp.maximum(up, 0.0)).astype(BF16)
    x2 = x1 + _dot(act, wdn_ref[...])
    out_ref[...] = _rms(x2, gfin_ref[...]) if final else x2


def _post(x, att, gates, yb, yc, lw, gfin, final, tm):
    m = x.shape[0]
    n_groups = len(att) // 2 if len(att) > 1 else 1
    row = lambda i: (i, 0)
    tok = lambda wd: pl.BlockSpec((tm, wd), row)
    return pl.pallas_call(
        functools.partial(_post_kernel, n_groups=n_groups, final=final),
        grid=(m // tm,),
        in_specs=[tok(D_MODEL)] + [tok(D_A)] * len(att) + [tok(3 * D_MODEL), tok(D_MODEL), tok(H_C * DV_C),
                  _resident((D_A, D_MODEL)), _resident((H_C * DV_C, D_MODEL)), _resident((D_MODEL, D_MODEL)),
                  _resident((1, D_MODEL)), _resident((D_MODEL, D_FF)), _resident((D_FF, D_MODEL)),
                  _resident((1, D_MODEL))],
        out_specs=tok(D_MODEL),
        out_shape=jax.ShapeDtypeStruct((m, D_MODEL), F32),
        compiler_params=_cparams(("parallel",)),
        name="post",
    )(x, *att, gates, yb, yc, lw["woa"], lw["woc"], lw["wout"], lw["gmlp"], lw["wup"], lw["wdn"], gfin)


def _t5_bucket(dist):
    exact = N_BUCKETS // 2
    d = np.maximum(dist, 1).astype(np.float32)
    large = exact + (np.log(d / exact) / np.log(W_MAX / exact) * (N_BUCKETS - exact)).astype(np.int32)
    large = np.minimum(large, N_BUCKETS - 1)
    return np.where(dist < exact, dist, large).astype(np.int32)


def _prompt_bias(rel_bias, g, dil):
    step = N_BACK + np.arange(N_BACK)[:, None] - np.arange(2 * N_BACK)[None, :]
    bk = _t5_bucket(np.clip(step, 0, N_BACK) * dil)
    return jnp.transpose(rel_bias[bk][:, :, g * H_A:(g + 1) * H_A], (2, 0, 1))


def _sample_bias(rel_bias, wbuf, t_new):
    dist = wbuf + np.arange(t_new)[:, None] - np.arange(wbuf + t_new)[None, :]
    bk = _t5_bucket(np.clip(dist, 0, W_MAX))
    tab = rel_bias[bk]
    per_g = [jnp.transpose(tab[:, :, g * H_A:(g + 1) * H_A], (2, 0, 1)) for g in range(G_A)]
    full = jnp.concatenate(per_g, axis=1)
    return full[:, :, :wbuf], full[:, :, wbuf:]


def _layer_weights(l, w_in, w_o_a, w_glu, w_fg2, b_fg, gla_norm, w_o_c, w_out, norm_mix, norm_mlp, w_up, w_down):
    wl = w_in[l]
    cuts = np.cumsum((0, G_A * D_A, D_A, D_A, D_B, H_C * DK_C, H_C * DK_C, H_C * DV_C, H_C * DV_C, FG_RANK,
                      3 * D_MODEL))
    segs = [wl[:, cuts[i]:cuts[i + 1]] for i in range(len(cuts) - 1)]
    segs[8] = jnp.pad(segs[8], ((0, 0), (0, FG_PAD - FG_RANK)))
    return dict(
        win=jnp.concatenate(segs, axis=1).astype(BF16),
        gmix=norm_mix[l].reshape(1, D_MODEL),
        wglu=w_glu[l].astype(BF16),
        wfg=jnp.pad(w_fg2[l], ((0, FG_PAD - FG_RANK), (0, 0))).astype(BF16),
        bfg=b_fg[l].reshape(1, H_C * DK_C),
        gnorm=gla_norm[l].reshape(1, DV_C),
        woa=w_o_a[l].astype(BF16), woc=w_o_c[l].astype(BF16), wout=w_out[l].astype(BF16),
        gmlp=norm_mlp[l].reshape(1, D_MODEL),
        wup=w_up[l].astype(BF16), wdn=w_down[l].astype(BF16),
    )


def kernel(x_prompt, x_sample, cache_k_win, cache_v_win, state_ssm_re, state_ssm_im, state_gla, rel_bias, norm_mix, w_in, w_o_a, s5_log_dt, s5_a_re, s5_a_im, s5_b_re, s5_b_im, s5_c_re, s5_c_im, s5_d, w_glu, w_fg2, b_fg, gla_norm, w_o_c, w_out, norm_mlp, w_up, w_down, norm_final):
    bp, sp, _ = x_prompt.shape
    bs, ts, _ = x_sample.shape
    wbuf = cache_k_win.shape[2]
    mp, ms = bp * sp, bs * ts
    gfin = norm_final.reshape(1, D_MODEL)
    kcache = cache_k_win.reshape(DEPTH, bs, wbuf, D_A)
    vcache = cache_v_win.reshape(DEPTH, bs, wbuf, D_A)
    pbias = [_prompt_bias(rel_bias, g, dil) for g, (_, dil) in enumerate(PATTERNS)]
    sbias_c, sbias_n = _sample_bias(rel_bias, wbuf, ts)

    xp = x_prompt.reshape(mp, D_MODEL)
    xs = x_sample.reshape(ms, D_MODEL)
    outs = {k: [] for k in ("kp", "vp", "ks", "vs", "rp", "ip", "rs", "is", "gp", "gs")}
    for l in range(DEPTH):
        lw = _layer_weights(l, w_in, w_o_a, w_glu, w_fg2, b_fg, gla_norm, w_o_c, w_out, norm_mix, norm_mlp,
                            w_up, w_down)
        s5p = _s5_params(s5_log_dt[l], s5_a_re[l], s5_a_im[l], s5_b_re[l], s5_b_im[l], s5_c_re[l], s5_c_im[l],
                         s5_d[l])
        final = l == DEPTH - 1

        qa, ka, va, ub, qc, kc, vc, rc, fc, gates = _in_proj(xp, lw["gmix"], lw["win"], 256)
        att = []
        for g, (_, dil) in enumerate(PATTERNS):
            att += list(_attn_prompt(qa, ka, va, pbias[g], g, dil, bp, sp))
        u_tb = ub.reshape(bp, sp, D_B).transpose(1, 0, 2).reshape(mp, D_B)
        zeros = jnp.zeros((bp, D_STATE), F32)
        yb_tb, xr, xi = _s5(u_tb, zeros, zeros, s5p, lw["wglu"], bp, 128)
        yb = yb_tb.reshape(sp, bp, D_MODEL).transpose(1, 0, 2).reshape(mp, D_MODEL)
        yc, gst = _gla(qc, kc, vc, rc, fc, lw["wfg"], lw["bfg"], lw["gnorm"],
                       jnp.zeros((bp, H_C, DK_C, DV_C), F32), bp, sp)
        xp = _post(xp, att, gates, yb, yc, lw, gfin, final, 256)
        outs["kp"].append(ka.reshape(bp, sp, H_A, HD_A)[:, -min(W_MAX, sp):])
        outs["vp"].append(va.reshape(bp, sp, H_A, HD_A)[:, -min(W_MAX, sp):])
        outs["rp"].append(xr.reshape(bp, G_B, N_B))
        outs["ip"].append(xi.reshape(bp, G_B, N_B))
        outs["gp"].append(gst)

        qa, ka, va, ub, qc, kc, vc, rc, fc, gates = _in_proj(xs, lw["gmix"], lw["win"], 256)
        q3 = qa.reshape(bs, ts, G_A, D_A).transpose(0, 2, 1, 3).reshape(bs, G_A * ts, D_A)
        o_a, nk, nv = _attn_sample(q3, kcache, vcache, l, ka.reshape(bs, ts, D_A), va.reshape(bs, ts, D_A),
                                   sbias_c, sbias_n)
        u_tb = ub.reshape(bs, ts, D_B).transpose(1, 0, 2).reshape(ms, D_B)
        yb_tb, xr, xi = _s5(u_tb, state_ssm_re[l].reshape(bs, D_STATE), state_ssm_im[l].reshape(bs, D_STATE),
                            s5p, lw["wglu"], bs, ts)
        yb = yb_tb.reshape(ts, bs, D_MODEL).transpose(1, 0, 2).reshape(ms, D_MODEL)
        yc, gst = _gla(qc, kc, vc, rc, fc, lw["wfg"], lw["bfg"], lw["gnorm"], state_gla[l], bs, ts)
        xs = _post(xs, [o_a.reshape(ms, D_A)], gates, yb, yc, lw, gfin, final, 256)
        outs["ks"].append(nk.reshape(bs, wbuf, H_A, HD_A))
        outs["vs"].append(nv.reshape(bs, wbuf, H_A, HD_A))
        outs["rs"].append(xr.reshape(bs, G_B, N_B))
        outs["is"].append(xi.reshape(bs, G_B, N_B))
        outs["gs"].append(gst)

    st = {k: jnp.stack(v) for k, v in outs.items()}
    return (xp.reshape(bp, sp, D_MODEL), xs.reshape(bs, ts, D_MODEL),
            st["kp"], st["vp"], st["ks"], st["vs"], st["rp"], st["ip"], st["rs"], st["is"], st["gp"], st["gs"])
```

```python
import functools
import math

import numpy as np
import jax
import jax.numpy as jnp
from jax import lax
from jax.experimental import pallas as pl
from jax.experimental.pallas import tpu as pltpu

F32 = jnp.float32
BF16 = jnp.bfloat16

LANES = 128
D_MODEL = 1024
DEPTH = 4
H_A = 8
HD_A = 64
D_A = H_A * HD_A
N_PAIR = D_A // LANES
PATTERNS = ((128, 1), (512, 4), (2048, 16))
G_A = len(PATTERNS)
N_BACK = 128
W_MAX = 2048
N_BUCKETS = 32
P_B = 16
D_B = 512
G_B = D_B // P_B
N_B = 64
D_STATE = G_B * N_B
H_C = 4
DK_C = 64
DV_C = 128
FG_RANK = 16
FG_PAD = 128
GLA_TAU = 16.0
GLA_CHUNK = 64
D_FF = 4 * D_MODEL
EPS = 1e-6
NEG = -1e30

SEG_WIDTHS = (G_A * D_A, D_A, D_A, D_B, H_C * DK_C, H_C * DK_C, H_C * DV_C, H_C * DV_C, FG_PAD, 3 * D_MODEL)
SEG_OFFS = tuple(int(v) for v in np.cumsum((0,) + SEG_WIDTHS))
IN_COLS_PAD = SEG_OFFS[-1]

VMEM_LIMIT = 56 * 1024 * 1024


def _cparams(sem):
    return pltpu.CompilerParams(dimension_semantics=sem, vmem_limit_bytes=VMEM_LIMIT)


def _resident(shape):
    nd = len(shape)
    return pl.BlockSpec(shape, lambda *_: (0,) * nd, pipeline_mode=pl.Buffered(1))


def _rms(x, g):
    return x * lax.rsqrt(jnp.mean(x * x, axis=-1, keepdims=True) + EPS) * g


def _dot(a, b):
    return jnp.dot(a, b, preferred_element_type=F32)


def _dot_nt(a, b):
    return lax.dot_general(a, b, (((1,), (1,)), ((), ())), preferred_element_type=F32)


def _dot_tn(a, b):
    return lax.dot_general(a, b, (((0,), (0,)), ((), ())), preferred_element_type=F32)


def _in_proj_kernel(x_ref, g_ref, w_ref, *out_refs):
    hb = _rms(x_ref[...], g_ref[...]).astype(BF16)
    for s, o_ref in enumerate(out_refs):
        z = _dot(hb, w_ref[:, SEG_OFFS[s]:SEG_OFFS[s + 1]])
        if s in (0, 4):
            z = z * 0.125
        o_ref[...] = z


def _in_proj(x, g, w, tm):
    m = x.shape[0]
    row = lambda i: (i, 0)
    return pl.pallas_call(
        _in_proj_kernel,
        grid=(m // tm,),
        in_specs=[pl.BlockSpec((tm, D_MODEL), row), _resident((1, D_MODEL)), _resident((D_MODEL, IN_COLS_PAD))],
        out_specs=[pl.BlockSpec((tm, wd), row) for wd in SEG_WIDTHS],
        out_shape=[jax.ShapeDtypeStruct((m, wd), F32) for wd in SEG_WIDTHS],
        compiler_params=_cparams(("parallel",)),
        name="in_proj",
    )(x, g, w)


ATTN_UNROLL = 4


def _attn_prompt_kernel(q0_ref, q1_ref, q2_ref, k_ref, v_ref, b0_ref, b1_ref, b2_ref, o_ref,
                        qs, k1, v1, k4, v4, k16, v16, tq, tk, tv, og, lg, uo, ul, *, seq):
    nu = seq // N_BACK
    c4 = seq // 4
    left = lax.broadcasted_iota(jnp.int32, (N_BACK, LANES), 1) < HD_A

    def put_q(g, u, q):
        qs[g, 2 * N_BACK * u:2 * N_BACK * u + N_BACK, :] = jnp.where(left, q, 0.0).astype(BF16)
        qs[g, 2 * N_BACK * u + N_BACK:2 * N_BACK * (u + 1), :] = jnp.where(left, 0.0, q).astype(BF16)

    zpad = jnp.zeros((N_BACK, LANES), BF16)
    for buf in (k1, v1, k4, v4):
        buf[0:N_BACK, :] = zpad
    k1[N_BACK:N_BACK + seq, :] = k_ref[...].astype(BF16)
    v1[N_BACK:N_BACK + seq, :] = v_ref[...].astype(BF16)
    for r in range(4):
        by4 = pl.ds(r, c4, stride=4)
        tk[r * c4:(r + 1) * c4, :] = k_ref[by4, :]
        tv[r * c4:(r + 1) * c4, :] = v_ref[by4, :]
        tq[r * c4:(r + 1) * c4, :] = q2_ref[by4, :]
    k4[N_BACK:N_BACK + seq, :] = tk[...].astype(BF16)
    v4[N_BACK:N_BACK + seq, :] = tv[...].astype(BF16)
    for u in range(nu):
        put_q(0, u, q0_ref[u * N_BACK:(u + 1) * N_BACK, :])
        put_q(1, u, q1_ref[pl.ds(u // 4 + 4 * N_BACK * (u % 4), N_BACK, stride=4), :])
        by16 = pl.ds((u % 4) * c4 + u // 4, N_BACK, stride=4)
        put_q(2, u, tq[by16, :])
        k16[u * N_BACK:(u + 1) * N_BACK, :] = tk[by16, :].astype(BF16)
        v16[u * N_BACK:(u + 1) * N_BACK, :] = tv[by16, :].astype(BF16)

    shp = (2 * N_BACK, 2 * N_BACK)
    qi = lax.broadcasted_iota(jnp.int32, shp, 0) & (N_BACK - 1)
    kk = lax.broadcasted_iota(jnp.int32, shp, 1)
    step = N_BACK + qi - kk
    band = (step >= 0) & (step <= N_BACK)
    newer = kk >= N_BACK
    tri = band[:, N_BACK:]

    def unit(u, carry, *, g, kbuf, vbuf, b_ref, class_blocks):
        q = qs[g, pl.ds(pl.multiple_of(u * 2 * N_BACK, 2 * N_BACK), 2 * N_BACK), :]
        k0 = pl.multiple_of(u * N_BACK, N_BACK)
        if class_blocks > 1:
            krows = pl.ds(k0, 2 * N_BACK)
            valid = band & (newer | ((u & (class_blocks - 1)) > 0))
            bias = b_ref[0]
        else:
            krows = pl.ds(k0, N_BACK)
            valid = tri
            bias = b_ref[0, :, N_BACK:]
        s = jnp.where(valid, _dot_nt(q, kbuf[krows, :]) + bias, NEG)
        mx = jnp.max(s, axis=-1, keepdims=True)
        p = jnp.exp(s - mx)
        den = jnp.sum(p, axis=-1, keepdims=True)
        o = _dot(p.astype(BF16), vbuf[krows, :]) / den
        lse = jnp.broadcast_to(mx + jnp.log(den), (2 * N_BACK, LANES))
        og[g, pl.ds(k0, N_BACK), :] = jnp.where(left, o[0:N_BACK], o[N_BACK:])
        lg[g, pl.ds(k0, N_BACK), :] = jnp.where(left, lse[0:N_BACK], lse[N_BACK:])
        return carry

    for g, (kbuf, vbuf, b_ref, class_blocks) in enumerate(((k1, v1, b0_ref, nu), (k4, v4, b1_ref, nu // 4),
                                                           (k16, v16, b2_ref, 1))):
        lax.fori_loop(0, nu, functools.partial(unit, g=g, kbuf=kbuf, vbuf=vbuf, b_ref=b_ref,
                                               class_blocks=class_blocks), 0, unroll=ATTN_UNROLL)

    for r in range(4):
        by4 = pl.ds(r, c4, stride=4)
        uo[0, by4, :] = og[1, r * c4:(r + 1) * c4, :]
        ul[0, by4, :] = lg[1, r * c4:(r + 1) * c4, :]
    for u in range(nu):
        by16 = pl.ds((u % 4) * c4 + u // 4, N_BACK, stride=4)
        tq[by16, :] = og[2, u * N_BACK:(u + 1) * N_BACK, :]
        tk[by16, :] = lg[2, u * N_BACK:(u + 1) * N_BACK, :]
    for r in range(4):
        by4 = pl.ds(r, c4, stride=4)
        uo[1, by4, :] = tq[r * c4:(r + 1) * c4, :]
        ul[1, by4, :] = tk[r * c4:(r + 1) * c4, :]

    def merge(i, carry):
        rows = pl.ds(pl.multiple_of(i * N_BACK, N_BACK), N_BACK)
        ls = [lg[0, rows, :], ul[0, rows, :], ul[1, rows, :]]
        mx = jnp.maximum(jnp.maximum(ls[0], ls[1]), ls[2])
        ws = [jnp.exp(l - mx) for l in ls]
        num = ws[0] * og[0, rows, :] + ws[1] * uo[0, rows, :] + ws[2] * uo[1, rows, :]
        o_ref[rows, :] = num / (ws[0] + ws[1] + ws[2])
        return carry

    lax.fori_loop(0, nu, merge, 0)


def _attn_prompt(qa, ka, va, biasmats, batch, seq):
    assert tuple(d for _, d in PATTERNS) == (1, 4, 16) and seq == 16 * N_BACK
    blk = (seq, LANES)
    qspec = lambda g: pl.BlockSpec(blk, lambda b, p: (b, g * N_PAIR + p))
    kspec = pl.BlockSpec(blk, lambda b, p: (b, p))
    bspec = pl.BlockSpec((1, 2 * N_BACK, 2 * N_BACK), lambda b, p: (p, 0, 0))
    pair_bias = [bm.reshape(N_PAIR, 2 * N_BACK, 2 * N_BACK) for bm in biasmats]
    act = lambda rows: pltpu.VMEM((rows, LANES), BF16)
    return pl.pallas_call(
        functools.partial(_attn_prompt_kernel, seq=seq),
        grid=(batch, N_PAIR),
        in_specs=[qspec(0), qspec(1), qspec(2), kspec, kspec, bspec, bspec, bspec],
        out_specs=kspec,
        out_shape=jax.ShapeDtypeStruct((batch * seq, D_A), F32),
        scratch_shapes=[pltpu.VMEM((G_A, 2 * seq, LANES), BF16),
                        act(N_BACK + seq), act(N_BACK + seq), act(N_BACK + seq), act(N_BACK + seq),
                        act(seq), act(seq),
                        pltpu.VMEM((seq, LANES), F32), pltpu.VMEM((seq, LANES), F32), pltpu.VMEM((seq, LANES), F32),
                        pltpu.VMEM((G_A, seq, LANES), F32), pltpu.VMEM((G_A, seq, LANES), F32),
                        pltpu.VMEM((2, seq, LANES), F32), pltpu.VMEM((2, seq, LANES), F32)],
        compiler_params=_cparams(("parallel", "parallel")),
        name="attn_prompt",
    )(qa, qa, qa, ka, va, *pair_bias)


def _attn_sample_kernel(q_ref, kt_ref, vt_ref, knt_ref, vnt_ref, bc_ref, bn_ref, o_ref, *, wbuf, t_new):
    rows = G_A * t_new

    def allowed(shape, key0):
        rr = lax.broadcasted_iota(jnp.int32, shape, 0)
        col = lax.broadcasted_iota(jnp.int32, shape, 1)
        grp = rr >> (t_new.bit_length() - 1)
        dist = wbuf + (rr - grp * t_new) - (col + key0)
        dm1 = jnp.where(grp == 0, PATTERNS[0][1] - 1, jnp.where(grp == 1, PATTERNS[1][1] - 1, PATTERNS[2][1] - 1))
        win = jnp.where(grp == 0, PATTERNS[0][0], jnp.where(grp == 1, PATTERNS[1][0], PATTERNS[2][0]))
        return (dist >= 0) & ((dist & dm1) == 0) & (dist <= win)

    ok_c = allowed((rows, wbuf), 0)
    ok_n = allowed((rows, t_new), wbuf)
    for h in range(H_A):
        qh = q_ref[0, h].astype(BF16)
        sc = jnp.where(ok_c, _dot(qh, kt_ref[0, 0, h].astype(BF16)) + bc_ref[h], NEG)
        sn = jnp.where(ok_n, _dot(qh, knt_ref[0, h].astype(BF16)) + bn_ref[h], NEG)
        mx = jnp.maximum(jnp.max(sc, axis=-1, keepdims=True), jnp.max(sn, axis=-1, keepdims=True))
        mt = jnp.maximum(jnp.maximum(mx[0:t_new], mx[t_new:2 * t_new]), mx[2 * t_new:3 * t_new])
        mall = jnp.concatenate([mt, mt, mt], axis=0)
        pc = jnp.exp(sc - mall)
        pn = jnp.exp(sn - mall)
        den = jnp.sum(pc, axis=-1, keepdims=True) + jnp.sum(pn, axis=-1, keepdims=True)
        acc = (_dot_nt(pc.astype(BF16), vt_ref[0, 0, h].astype(BF16))
               + _dot_nt(pn.astype(BF16), vnt_ref[0, h].astype(BF16)))
        dt = den[0:t_new] + den[t_new:2 * t_new] + den[2 * t_new:3 * t_new]
        at = acc[0:t_new] + acc[t_new:2 * t_new] + acc[2 * t_new:3 * t_new]
        o_ref[0, h] = at / dt


def _attn_sample(q, kt_cache, vt_cache, layer, knt, vnt, bias_c, bias_n):
    batch, wbuf = kt_cache.shape[1], kt_cache.shape[4]
    t_new = knt.shape[3]
    rows = G_A * t_new
    b4 = lambda b: (b, 0, 0, 0)
    cache_spec = pl.BlockSpec((1, 1, H_A, HD_A, wbuf), lambda b: (layer, b, 0, 0, 0))
    new_spec = pl.BlockSpec((1, H_A, HD_A, t_new), b4)
    return pl.pallas_call(
        functools.partial(_attn_sample_kernel, wbuf=wbuf, t_new=t_new),
        grid=(batch,),
        in_specs=[pl.BlockSpec((1, H_A, rows, HD_A), b4), cache_spec, cache_spec, new_spec, new_spec,
                  _resident((H_A, rows, wbuf)), _resident((H_A, rows, t_new))],
        out_specs=pl.BlockSpec((1, H_A, t_new, HD_A), b4),
        out_shape=jax.ShapeDtypeStruct((batch, H_A, t_new, HD_A), F32),
        compiler_params=_cparams(("parallel",)),
        name="attn_sample",
    )(q, kt_cache, vt_cache, knt, vnt, bias_c, bias_n)


def _cache_update_kernel(kt_ref, vt_ref, kn_ref, vn_ref, ko_ref, vo_ref, *, wbuf, t_new):
    is_new = lax.broadcasted_iota(jnp.int32, (HD_A, LANES), 1) >= LANES - t_new
    for src, new, dst in ((kt_ref, kn_ref, ko_ref), (vt_ref, vn_ref, vo_ref)):
        for h in range(H_A):
            moved = pltpu.roll(src[0, 0, h], wbuf - t_new, 1)
            dst[0, 0, h, :, 0:wbuf - LANES] = moved[:, 0:wbuf - LANES]
            dst[0, 0, h, :, wbuf - LANES:wbuf] = jnp.where(is_new, new[0, 0, h], moved[:, wbuf - LANES:wbuf])


def _cache_update(kt_cache, vt_cache, kn_tail, vn_tail, t_new):
    depth, batch, _, _, wbuf = kt_cache.shape
    idx = lambda l, b: (l, b, 0, 0, 0)
    cache_spec = pl.BlockSpec((1, 1, H_A, HD_A, wbuf), idx)
    tail_spec = pl.BlockSpec((1, 1, H_A, HD_A, LANES), idx)
    shape = jax.ShapeDtypeStruct(kt_cache.shape, F32)
    return pl.pallas_call(
        functools.partial(_cache_update_kernel, wbuf=wbuf, t_new=t_new),
        grid=(depth, batch),
        in_specs=[cache_spec, cache_spec, tail_spec, tail_spec],
        out_specs=[cache_spec, cache_spec],
        out_shape=[shape, shape],
        compiler_params=_cparams(("parallel", "parallel")),
        name="cache_update",
    )(kt_cache, vt_cache, kn_tail, vn_tail)


S5_KB = 128
S5_NBLK = D_B // S5_KB
S5_SB = D_STATE // S5_NBLK


def _s5_kernel(u_ref, x0r_ref, x0i_ref, ar_ref, ai_ref, bre_ref, bim_ref, cre_ref, cim_ref, d_ref, wglu_ref,
               yb_ref, xr_out, xi_out, bur, bui, sr, si, *, nb, tc):
    @pl.when(pl.program_id(0) == 0)
    def _():
        sr[...] = x0r_ref[...]
        si[...] = x0i_ref[...]

    u = u_ref[...]
    ub = u.astype(BF16)
    for j in range(S5_NBLK):
        uj = ub[:, j * S5_KB:(j + 1) * S5_KB]
        bur[:, j * S5_SB:(j + 1) * S5_SB] = _dot(uj, bre_ref[j])
        bui[:, j * S5_SB:(j + 1) * S5_SB] = _dot(uj, bim_ref[j])

    ar = ar_ref[...]
    ai = ai_ref[...]

    def step(t, carry):
        xr, xi = carry
        rows = pl.ds(pl.multiple_of(t * nb, nb), nb)
        nr = ar * xr - ai * xi + bur[rows, :]
        ni = ar * xi + ai * xr + bui[rows, :]
        bur[rows, :] = nr
        bui[rows, :] = ni
        return nr, ni

    xr, xi = lax.fori_loop(0, tc, step, (sr[...], si[...]))
    sr[...] = xr
    si[...] = xi
    xr_out[...] = xr
    xi_out[...] = xi

    ys = []
    for j in range(S5_NBLK):
        cs = slice(j * S5_SB, (j + 1) * S5_SB)
        ys.append(_dot(bur[:, cs].astype(BF16), cre_ref[j]) - _dot(bui[:, cs].astype(BF16), cim_ref[j]))
    y = jnp.concatenate(ys, axis=-1) + d_ref[...] * u
    ab = _dot(jax.nn.gelu(y).astype(BF16), wglu_ref[...])
    yb_ref[...] = ab[:, :D_MODEL] * jax.nn.sigmoid(ab[:, D_MODEL:])


def _s5(u_tb, x0r, x0i, prm, wglu, nb, tc):
    rows = u_tb.shape[0]
    steps = rows // (tc * nb)
    r = tc * nb
    row = lambda i: (i, 0)
    st = pl.BlockSpec((nb, D_STATE), lambda i: (0, 0))
    return pl.pallas_call(
        functools.partial(_s5_kernel, nb=nb, tc=tc),
        grid=(steps,),
        in_specs=[pl.BlockSpec((r, D_B), row), st, st,
                  _resident((1, D_STATE)), _resident((1, D_STATE)),
                  _resident((S5_NBLK, S5_KB, S5_SB)), _resident((S5_NBLK, S5_KB, S5_SB)),
                  _resident((S5_NBLK, S5_SB, S5_KB)), _resident((S5_NBLK, S5_SB, S5_KB)),
                  _resident((1, D_B)), _resident((D_B, 2 * D_MODEL))],
        out_specs=[pl.BlockSpec((r, D_MODEL), row), st, st],
        out_shape=[jax.ShapeDtypeStruct((rows, D_MODEL), F32),
                   jax.ShapeDtypeStruct((nb, D_STATE), F32), jax.ShapeDtypeStruct((nb, D_STATE), F32)],
        scratch_shapes=[pltpu.VMEM((r, D_STATE), F32), pltpu.VMEM((r, D_STATE), F32),
                        pltpu.VMEM((nb, D_STATE), F32), pltpu.VMEM((nb, D_STATE), F32)],
        compiler_params=_cparams(("arbitrary",)),
        name="s5",
    )(u_tb, x0r, x0i, prm["ar"], prm["ai"], prm["bre"], prm["bim"], prm["cre"], prm["cim"], prm["d"], wglu)


def _s5_params(log_dt, a_re, a_im, b_re, b_im, c_re, c_im, d_skip):
    dt = jnp.exp(log_dt)[:, None]
    mag = jnp.exp(a_re * dt)
    ab_re = mag * jnp.cos(a_im * dt)
    ab_im = mag * jnp.sin(a_im * dt)
    den = a_re * a_re + a_im * a_im
    nr = ab_re - 1.0
    co_re = (nr * a_re + ab_im * a_im) / den
    co_im = (ab_im * a_re - nr * a_im) / den
    bb_re = co_re[..., None] * b_re - co_im[..., None] * b_im
    bb_im = co_re[..., None] * b_im + co_im[..., None] * b_re
    gpb = G_B // S5_NBLK
    eye = jnp.eye(gpb, dtype=F32)

    def pack_b(bb):
        t = bb.reshape(S5_NBLK, gpb, N_B, P_B)
        m = jnp.einsum('jgnp,gh->jgphn', t, eye)
        return m.reshape(S5_NBLK, gpb * P_B, gpb * N_B).astype(BF16)

    def pack_c(cc):
        t = cc.reshape(S5_NBLK, gpb, P_B, N_B)
        m = jnp.einsum('jgpn,gh->jgnhp', t, eye)
        return m.reshape(S5_NBLK, gpb * N_B, gpb * P_B).astype(BF16)

    return dict(ar=ab_re.reshape(1, D_STATE), ai=ab_im.reshape(1, D_STATE),
                bre=pack_b(bb_re), bim=pack_b(bb_im), cre=pack_c(c_re), cim=pack_c(c_im),
                d=d_skip.reshape(1, D_B))


def _gla_kernel(q_ref, k_ref, v_ref, r_ref, fc_ref, wfg_ref, bfg_ref, gn_ref, s0_ref, y_ref, sfin_ref, st,
                *, chunk, nchunk):
    tb = pl.program_id(1)

    @pl.when(tb == 0)
    def _():
        for h in range(H_C):
            st[h] = s0_ref[0, h].T

    rows_blk = chunk * nchunk
    la = jax.nn.log_sigmoid(_dot(fc_ref[...].astype(BF16), wfg_ref[...]) + bfg_ref[...]) / GLA_TAU
    ri = lax.broadcasted_iota(jnp.int32, (rows_blk, rows_blk), 0)
    ci = lax.broadcasted_iota(jnp.int32, (rows_blk, rows_blk), 1)
    sh = chunk.bit_length() - 1
    same = (ri >> sh) == (ci >> sh)
    tri = jnp.where(same & (ci <= ri), 1.0, 0.0).astype(BF16)
    la_hi = la.astype(BF16)
    la_lo = (la - la_hi.astype(F32)).astype(BF16)
    bcum = _dot(tri, la_hi) + _dot(tri, la_lo)
    causal = lax.broadcasted_iota(jnp.int32, (chunk, chunk), 1) <= lax.broadcasted_iota(jnp.int32, (chunk, chunk), 0)
    gn = gn_ref[...]
    for c in range(nchunk):
        rs = slice(c * chunk, (c + 1) * chunk)
        b = bcum[rs]
        bl = b[chunk - 1:chunk]
        qt = (q_ref[rs, :] * jnp.exp(b)).astype(BF16)
        kt = (k_ref[rs, :] * jnp.exp(-b)).astype(BF16)
        kd = (k_ref[rs, :] * jnp.exp(bl - b)).astype(BF16)
        dec = jnp.exp(bl)
        vb = v_ref[rs, :].astype(BF16)
        for h in range(H_C):
            ks = slice(h * DK_C, (h + 1) * DK_C)
            vs = slice(h * DV_C, (h + 1) * DV_C)
            s_t = st[h]
            att = jnp.where(causal, _dot_nt(qt[:, ks], kt[:, ks]), 0.0)
            o = _dot(att.astype(BF16), vb[:, vs]) + _dot_nt(qt[:, ks], s_t.astype(BF16))
            st[h] = s_t * dec[:, ks] + _dot_tn(vb[:, vs], kd[:, ks])
            y_ref[rs, vs] = _rms(o, gn) * jax.nn.silu(r_ref[rs, vs])

    @pl.when(tb == pl.num_programs(1) - 1)
    def _():
        for h in range(H_C):
            sfin_ref[0, h] = st[h].T


def _gla(qc, kc, vc, rc, fc, wfg, bfg, gnorm, s0, batch, seq):
    chunk = math.gcd(seq, GLA_CHUNK)
    nchunk = min(seq // chunk, 4)
    rows_blk = chunk * nchunk
    nt = seq // rows_blk
    row = lambda b, t: (b * nt + t, 0)
    st_spec = pl.BlockSpec((1, H_C, DK_C, DV_C), lambda b, t: (b, 0, 0, 0))
    return pl.pallas_call(
        functools.partial(_gla_kernel, chunk=chunk, nchunk=nchunk),
        grid=(batch, nt),
        in_specs=[pl.BlockSpec((rows_blk, H_C * DK_C), row), pl.BlockSpec((rows_blk, H_C * DK_C), row),
                  pl.BlockSpec((rows_blk, H_C * DV_C), row), pl.BlockSpec((rows_blk, H_C * DV_C), row),
                  pl.BlockSpec((rows_blk, FG_PAD), row),
                  _resident((FG_PAD, H_C * DK_C)), _resident((1, H_C * DK_C)), _resident((1, DV_C)), st_spec],
        out_specs=[pl.BlockSpec((rows_blk, H_C * DV_C), row), st_spec],
        out_shape=[jax.ShapeDtypeStruct((batch * seq, H_C * DV_C), F32),
                   jax.ShapeDtypeStruct((batch, H_C, DK_C, DV_C), F32)],
        scratch_shapes=[pltpu.VMEM((H_C, DV_C, DK_C), F32)],
        compiler_params=_cparams(("parallel", "arbitrary")),
        name="gla",
    )(qc, kc, vc, rc, fc, wfg, bfg, gnorm, s0)


def _post_kernel(x_ref, oa_ref, gates_ref, yb_ref, yc_ref, woa_ref, woc_ref, wout_ref, gmlp_ref, wup_ref, wdn_ref,
                 gfin_ref, out_ref, *, final):
    y_a = _dot(oa_ref[...].astype(BF16), woa_ref[...])
    y_c = _dot(yc_ref[...].astype(BF16), woc_ref[...])
    gates = gates_ref[...]
    m = (jax.nn.sigmoid(gates[:, :D_MODEL]) * y_a + jax.nn.sigmoid(gates[:, D_MODEL:2 * D_MODEL]) * yb_ref[...]
         + jax.nn.sigmoid(gates[:, 2 * D_MODEL:]) * y_c)
    x1 = x_ref[...] + _dot(m.astype(BF16), wout_ref[...])
    hm = _rms(x1, gmlp_ref[...]).astype(BF16)
    up = _dot(hm, wup_ref[...])
    act = jnp.square(jnp.maximum(up, 0.0)).astype(BF16)
    x2 = x1 + _dot(act, wdn_ref[...])
    out_ref[...] = _rms(x2, gfin_ref[...]) if final else x2


def _post(x, oa, gates, yb, yc, lw, gfin, final, tm):
    m = x.shape[0]
    row = lambda i: (i, 0)
    tok = lambda wd: pl.BlockSpec((tm, wd), row)
    return pl.pallas_call(
        functools.partial(_post_kernel, final=final),
        grid=(m // tm,),
        in_specs=[tok(D_MODEL), tok(D_A), tok(3 * D_MODEL), tok(D_MODEL), tok(H_C * DV_C),
                  _resident((D_A, D_MODEL)), _resident((H_C * DV_C, D_MODEL)), _resident((D_MODEL, D_MODEL)),
                  _resident((1, D_MODEL)), _resident((D_MODEL, D_FF)), _resident((D_FF, D_MODEL)),
                  _resident((1, D_MODEL))],
        out_specs=tok(D_MODEL),
        out_shape=jax.ShapeDtypeStruct((m, D_MODEL), F32),
        compiler_params=_cparams(("parallel",)),
        name="post",
    )(x, oa, gates, yb, yc, lw["woa"], lw["woc"], lw["wout"], lw["gmlp"], lw["wup"], lw["wdn"], gfin)


def _t5_bucket(dist):
    exact = N_BUCKETS // 2
    d = np.maximum(dist, 1).astype(np.float32)
    large = exact + (np.log(d / exact) / np.log(W_MAX / exact) * (N_BUCKETS - exact)).astype(np.int32)
    large = np.minimum(large, N_BUCKETS - 1)
    return np.where(dist < exact, dist, large).astype(np.int32)


def _toeplitz(u, rows, cols, off):
    period = cols + off + 1
    w = jnp.pad(u, ((0, 0), (0, period - u.shape[1])))
    a = jnp.tile(w, (1, rows))[:, :rows * (period - 1)].reshape(u.shape[0], rows, period - 1)
    return a[:, :, off:off + cols]


def _prompt_bias(rel_bias, g, dil):
    x = np.arange(3 * N_BACK - 1)
    bk = _t5_bucket(np.clip(2 * N_BACK - 1 - x, 0, N_BACK) * dil)
    u = rel_bias[bk][:, g * H_A:(g + 1) * H_A].T
    return _toeplitz(u, N_BACK, 2 * N_BACK, N_BACK - 1)


def _sample_bias(rel_bias, wbuf, t_new):
    n = wbuf + t_new
    x = np.arange(n + t_new - 1)
    bk = _t5_bucket(np.clip(n - 1 - x, 0, W_MAX))
    tab = rel_bias[bk]
    per_g = [_toeplitz(tab[:, g * H_A:(g + 1) * H_A].T, t_new, n, t_new - 1) for g in range(G_A)]
    full = jnp.concatenate(per_g, axis=1)
    return full[:, :, :wbuf], full[:, :, wbuf:]


def _layer_weights(l, w_in, w_o_a, w_glu, w_fg2, b_fg, gla_norm, w_o_c, w_out, norm_mix, norm_mlp, w_up, w_down):
    wl = w_in[l]
    cuts = np.cumsum((0, G_A * D_A, D_A, D_A, D_B, H_C * DK_C, H_C * DK_C, H_C * DV_C, H_C * DV_C, FG_RANK,
                      3 * D_MODEL))
    segs = [wl[:, cuts[i]:cuts[i + 1]] for i in range(len(cuts) - 1)]
    segs[8] = jnp.pad(segs[8], ((0, 0), (0, FG_PAD - FG_RANK)))
    return dict(
        win=jnp.concatenate(segs, axis=1).astype(BF16),
        gmix=norm_mix[l].reshape(1, D_MODEL),
        wglu=w_glu[l].astype(BF16),
        wfg=jnp.pad(w_fg2[l], ((0, FG_PAD - FG_RANK), (0, 0))).astype(BF16),
        bfg=b_fg[l].reshape(1, H_C * DK_C),
        gnorm=gla_norm[l].reshape(1, DV_C),
        woa=w_o_a[l].astype(BF16), woc=w_o_c[l].astype(BF16), wout=w_out[l].astype(BF16),
        gmlp=norm_mlp[l].reshape(1, D_MODEL),
        wup=w_up[l].astype(BF16), wdn=w_down[l].astype(BF16),
    )


def kernel(x_prompt, x_sample, cache_k_win, cache_v_win, state_ssm_re, state_ssm_im, state_gla, rel_bias, norm_mix, w_in, w_o_a, s5_log_dt, s5_a_re, s5_a_im, s5_b_re, s5_b_im, s5_c_re, s5_c_im, s5_d, w_glu, w_fg2, b_fg, gla_norm, w_o_c, w_out, norm_mlp, w_up, w_down, norm_final):
    bp, sp, _ = x_prompt.shape
    bs, ts, _ = x_sample.shape
    wbuf = cache_k_win.shape[2]
    mp, ms = bp * sp, bs * ts
    gfin = norm_final.reshape(1, D_MODEL)
    kt_cache = jnp.transpose(cache_k_win, (0, 1, 3, 4, 2))
    vt_cache = jnp.transpose(cache_v_win, (0, 1, 3, 4, 2))
    pbias = [_prompt_bias(rel_bias, g, dil) for g, (_, dil) in enumerate(PATTERNS)]
    sbias_c, sbias_n = _sample_bias(rel_bias, wbuf, ts)

    xp = x_prompt.reshape(mp, D_MODEL)
    xs = x_sample.reshape(ms, D_MODEL)
    outs = {k: [] for k in ("kp", "vp", "kn", "vn", "rp", "ip", "rs", "is", "gp", "gs")}
    for l in range(DEPTH):
        lw = _layer_weights(l, w_in, w_o_a, w_glu, w_fg2, b_fg, gla_norm, w_o_c, w_out, norm_mix, norm_mlp,
                            w_up, w_down)
        s5p = _s5_params(s5_log_dt[l], s5_a_re[l], s5_a_im[l], s5_b_re[l], s5_b_im[l], s5_c_re[l], s5_c_im[l],
                         s5_d[l])
        final = l == DEPTH - 1

        qa, ka, va, ub, qc, kc, vc, rc, fc, gates = _in_proj(xp, lw["gmix"], lw["win"], 256)
        oa = _attn_prompt(qa, ka, va, pbias, bp, sp)
        u_tb = ub.reshape(bp, sp, D_B).transpose(1, 0, 2).reshape(mp, D_B)
        zeros = jnp.zeros((bp, D_STATE), F32)
        yb_tb, xr, xi = _s5(u_tb, zeros, zeros, s5p, lw["wglu"], bp, 128)
        yb = yb_tb.reshape(sp, bp, D_MODEL).transpose(1, 0, 2).reshape(mp, D_MODEL)
        yc, gst = _gla(qc, kc, vc, rc, fc, lw["wfg"], lw["bfg"], lw["gnorm"],
                       jnp.zeros((bp, H_C, DK_C, DV_C), F32), bp, sp)
        xp = _post(xp, oa, gates, yb, yc, lw, gfin, final, 256)
        outs["kp"].append(ka.reshape(bp, sp, H_A, HD_A)[:, -min(W_MAX, sp):])
        outs["vp"].append(va.reshape(bp, sp, H_A, HD_A)[:, -min(W_MAX, sp):])
        outs["rp"].append(xr.reshape(bp, G_B, N_B))
        outs["ip"].append(xi.reshape(bp, G_B, N_B))
        outs["gp"].append(gst)

        qa, ka, va, ub, qc, kc, vc, rc, fc, gates = _in_proj(xs, lw["gmix"], lw["win"], 256)
        q4 = qa.reshape(bs, ts, G_A, H_A, HD_A).transpose(0, 3, 2, 1, 4).reshape(bs, H_A, G_A * ts, HD_A)
        knt = ka.reshape(bs, ts, H_A, HD_A).transpose(0, 2, 3, 1)
        vnt = va.reshape(bs, ts, H_A, HD_A).transpose(0, 2, 3, 1)
        o4 = _attn_sample(q4, kt_cache, vt_cache, l, knt, vnt, sbias_c, sbias_n)
        oa = o4.transpose(0, 2, 1, 3).reshape(ms, D_A)
        u_tb = ub.reshape(bs, ts, D_B).transpose(1, 0, 2).reshape(ms, D_B)
        yb_tb, xr, xi = _s5(u_tb, state_ssm_re[l].reshape(bs, D_STATE), state_ssm_im[l].reshape(bs, D_STATE),
                            s5p, lw["wglu"], bs, ts)
        yb = yb_tb.reshape(ts, bs, D_MODEL).transpose(1, 0, 2).reshape(ms, D_MODEL)
        yc, gst = _gla(qc, kc, vc, rc, fc, lw["wfg"], lw["bfg"], lw["gnorm"], state_gla[l], bs, ts)
        xs = _post(xs, oa, gates, yb, yc, lw, gfin, final, 256)
        outs["kn"].append(knt)
        outs["vn"].append(vnt)
        outs["rs"].append(xr.reshape(bs, G_B, N_B))
        outs["is"].append(xi.reshape(bs, G_B, N_B))
        outs["gs"].append(gst)

    st = {k: jnp.stack(v) for k, v in outs.items()}
    tail = ((0, 0), (0, 0), (0, 0), (0, 0), (LANES - ts, 0))
    kt_new, vt_new = _cache_update(kt_cache, vt_cache, jnp.pad(st["kn"], tail), jnp.pad(st["vn"], tail), ts)
    ks = jnp.transpose(kt_new, (0, 1, 4, 2, 3))
    vs = jnp.transpose(vt_new, (0, 1, 4, 2, 3))
    return (xp.reshape(bp, sp, D_MODEL), xs.reshape(bs, ts, D_MODEL),
            st["kp"], st["vp"], ks, vs, st["rp"], st["ip"], st["rs"], st["is"], st["gp"], st["gs"])
```

```python
import functools
import math

import numpy as np
import jax
import jax.numpy as jnp
from jax import lax
from jax.experimental import pallas as pl
from jax.experimental.pallas import tpu as pltpu

F32 = jnp.float32
BF16 = jnp.bfloat16

LANES = 128
D_MODEL = 1024
DEPTH = 4
H_A = 8
HD_A = 64
D_A = H_A * HD_A
N_PAIR = D_A // LANES
PATTERNS = ((128, 1), (512, 4), (2048, 16))
G_A = len(PATTERNS)
N_BACK = 128
W_MAX = 2048
N_BUCKETS = 32
P_B = 16
D_B = 512
G_B = D_B // P_B
N_B = 64
D_STATE = G_B * N_B
H_C = 4
DK_C = 64
DV_C = 128
FG_RANK = 16
FG_PAD = 128
GLA_TAU = 16.0
GLA_CHUNK = 64
D_FF = 4 * D_MODEL
EPS = 1e-6
NEG = -1e30

SEG_WIDTHS = (G_A * D_A, D_A, D_A, D_B, H_C * DK_C, H_C * DK_C, H_C * DV_C, H_C * DV_C, FG_PAD, 3 * D_MODEL)
SEG_OFFS = tuple(int(v) for v in np.cumsum((0,) + SEG_WIDTHS))
IN_COLS_PAD = SEG_OFFS[-1]

VMEM_LIMIT = 56 * 1024 * 1024


def _cparams(sem):
    return pltpu.CompilerParams(dimension_semantics=sem, vmem_limit_bytes=VMEM_LIMIT)


def _resident(shape):
    nd = len(shape)
    return pl.BlockSpec(shape, lambda *_: (0,) * nd, pipeline_mode=pl.Buffered(1))


def _rms(x, g):
    return x * lax.rsqrt(jnp.mean(x * x, axis=-1, keepdims=True) + EPS) * g


def _dot(a, b):
    return jnp.dot(a, b, preferred_element_type=F32)


def _dot_nt(a, b):
    return lax.dot_general(a, b, (((1,), (1,)), ((), ())), preferred_element_type=F32)


def _dot_tn(a, b):
    return lax.dot_general(a, b, (((0,), (0,)), ((), ())), preferred_element_type=F32)


def _in_proj_kernel(x_ref, g_ref, w_ref, *out_refs):
    hb = _rms(x_ref[...], g_ref[...]).astype(BF16)
    for s, o_ref in enumerate(out_refs):
        z = _dot(hb, w_ref[:, SEG_OFFS[s]:SEG_OFFS[s + 1]])
        if s in (0, 4):
            z = z * 0.125
        o_ref[...] = z


def _in_proj(x, g, w, tm):
    m = x.shape[0]
    row = lambda i: (i, 0)
    return pl.pallas_call(
        _in_proj_kernel,
        grid=(m // tm,),
        in_specs=[pl.BlockSpec((tm, D_MODEL), row), _resident((1, D_MODEL)), _resident((D_MODEL, IN_COLS_PAD))],
        out_specs=[pl.BlockSpec((tm, wd), row) for wd in SEG_WIDTHS],
        out_shape=[jax.ShapeDtypeStruct((m, wd), F32) for wd in SEG_WIDTHS],
        compiler_params=_cparams(("parallel",)),
        name="in_proj",
    )(x, g, w)


ATTN_UNROLL = 8


def _attn_prompt_kernel(q0_ref, q1_ref, q2_ref, k_ref, v_ref, b0_ref, b1_ref, b2_ref, o_ref,
                        qs, k1, v1, k4, v4, k16, v16, tq, tk, tv, og, lg, uo, ul, *, seq):
    nu = seq // N_BACK
    c4 = seq // 4
    left = lax.broadcasted_iota(jnp.int32, (N_BACK, LANES), 1) < HD_A

    def put_q(g, u, q):
        qs[g, 2 * N_BACK * u:2 * N_BACK * u + N_BACK, :] = jnp.where(left, q, 0.0).astype(BF16)
        qs[g, 2 * N_BACK * u + N_BACK:2 * N_BACK * (u + 1), :] = jnp.where(left, 0.0, q).astype(BF16)

    zpad = jnp.zeros((N_BACK, LANES), BF16)
    for buf in (k1, k4, v1, v4):
        buf[0:N_BACK, 0:LANES] = zpad
    for buf in (v1, v4, v16):
        buf[:, LANES:] = jnp.ones((buf.shape[0], LANES), BF16)
    k1[N_BACK:N_BACK + seq, :] = k_ref[...].astype(BF16)
    v1[N_BACK:N_BACK + seq, 0:LANES] = v_ref[...].astype(BF16)
    for r in range(4):
        by4 = pl.ds(r, c4, stride=4)
        tk[r * c4:(r + 1) * c4, :] = k_ref[by4, :]
        tv[r * c4:(r + 1) * c4, :] = v_ref[by4, :]
        tq[r * c4:(r + 1) * c4, :] = q2_ref[by4, :]
    k4[N_BACK:N_BACK + seq, :] = tk[...].astype(BF16)
    v4[N_BACK:N_BACK + seq, 0:LANES] = tv[...].astype(BF16)
    for u in range(nu):
        put_q(0, u, q0_ref[u * N_BACK:(u + 1) * N_BACK, :])
        put_q(1, u, q1_ref[pl.ds(u // 4 + 4 * N_BACK * (u % 4), N_BACK, stride=4), :])
        by16 = pl.ds((u % 4) * c4 + u // 4, N_BACK, stride=4)
        put_q(2, u, tq[by16, :])
        k16[u * N_BACK:(u + 1) * N_BACK, :] = tk[by16, :].astype(BF16)
        v16[u * N_BACK:(u + 1) * N_BACK, 0:LANES] = tv[by16, :].astype(BF16)

    shp = (2 * N_BACK, 2 * N_BACK)
    qi = lax.broadcasted_iota(jnp.int32, shp, 0) & (N_BACK - 1)
    kk = lax.broadcasted_iota(jnp.int32, shp, 1)
    step = N_BACK + qi - kk
    band = (step >= 0) & (step <= N_BACK)
    newer = kk >= N_BACK
    tri = band[:, N_BACK:]

    def unit(u, carry, *, g, kbuf, vbuf, b_ref, class_blocks):
        q = qs[g, pl.ds(pl.multiple_of(u * 2 * N_BACK, 2 * N_BACK), 2 * N_BACK), :]
        k0 = pl.multiple_of(u * N_BACK, N_BACK)
        if class_blocks > 1:
            krows = pl.ds(k0, 2 * N_BACK)
            valid = band & (newer | ((u & (class_blocks - 1)) > 0))
            bias = b_ref[0]
        else:
            krows = pl.ds(k0, N_BACK)
            valid = tri
            bias = b_ref[0, :, N_BACK:]
        s = jnp.where(valid, _dot_nt(q, kbuf[krows, :]) + bias, NEG)
        mx = jnp.max(s, axis=-1, keepdims=True)
        p = jnp.exp(s - mx)
        pv = _dot(p.astype(BF16), vbuf[krows, :])
        den = pv[:, LANES:]
        o = pv[:, 0:LANES] / den
        lse = mx + jnp.log(den)
        og[g, pl.ds(k0, N_BACK), :] = jnp.where(left, o[0:N_BACK], o[N_BACK:])
        lg[g, pl.ds(k0, N_BACK), :] = jnp.where(left, lse[0:N_BACK], lse[N_BACK:])
        return carry

    for g, (kbuf, vbuf, b_ref, class_blocks) in enumerate(((k1, v1, b0_ref, nu), (k4, v4, b1_ref, nu // 4),
                                                           (k16, v16, b2_ref, 1))):
        lax.fori_loop(0, nu, functools.partial(unit, g=g, kbuf=kbuf, vbuf=vbuf, b_ref=b_ref,
                                               class_blocks=class_blocks), 0, unroll=ATTN_UNROLL)

    for r in range(4):
        by4 = pl.ds(r, c4, stride=4)
        uo[0, by4, :] = og[1, r * c4:(r + 1) * c4, :]
        ul[0, by4, :] = lg[1, r * c4:(r + 1) * c4, :]
    for u in range(nu):
        by16 = pl.ds((u % 4) * c4 + u // 4, N_BACK, stride=4)
        tq[by16, :] = og[2, u * N_BACK:(u + 1) * N_BACK, :]
        tk[by16, :] = lg[2, u * N_BACK:(u + 1) * N_BACK, :]
    for r in range(4):
        by4 = pl.ds(r, c4, stride=4)
        uo[1, by4, :] = tq[r * c4:(r + 1) * c4, :]
        ul[1, by4, :] = tk[r * c4:(r + 1) * c4, :]

    def merge(i, carry):
        rows = pl.ds(pl.multiple_of(i * N_BACK, N_BACK), N_BACK)
        ls = [lg[0, rows, :], ul[0, rows, :], ul[1, rows, :]]
        mx = jnp.maximum(jnp.maximum(ls[0], ls[1]), ls[2])
        ws = [jnp.exp(l - mx) for l in ls]
        num = ws[0] * og[0, rows, :] + ws[1] * uo[0, rows, :] + ws[2] * uo[1, rows, :]
        o_ref[rows, :] = num / (ws[0] + ws[1] + ws[2])
        return carry

    lax.fori_loop(0, nu, merge, 0)


def _attn_prompt(qa, ka, va, biasmats, batch, seq):
    assert tuple(d for _, d in PATTERNS) == (1, 4, 16) and seq == 16 * N_BACK
    blk = (seq, LANES)
    qspec = lambda g: pl.BlockSpec(blk, lambda b, p: (b, g * N_PAIR + p))
    kspec = pl.BlockSpec(blk, lambda b, p: (b, p))
    bspec = pl.BlockSpec((1, 2 * N_BACK, 2 * N_BACK), lambda b, p: (p, 0, 0))
    pair_bias = [bm.reshape(N_PAIR, 2 * N_BACK, 2 * N_BACK) for bm in biasmats]
    act = lambda rows: pltpu.VMEM((rows, LANES), BF16)
    val = lambda rows: pltpu.VMEM((rows, 2 * LANES), BF16)
    return pl.pallas_call(
        functools.partial(_attn_prompt_kernel, seq=seq),
        grid=(batch, N_PAIR),
        in_specs=[qspec(0), qspec(1), qspec(2), kspec, kspec, bspec, bspec, bspec],
        out_specs=kspec,
        out_shape=jax.ShapeDtypeStruct((batch * seq, D_A), F32),
        scratch_shapes=[pltpu.VMEM((G_A, 2 * seq, LANES), BF16),
                        act(N_BACK + seq), val(N_BACK + seq), act(N_BACK + seq), val(N_BACK + seq),
                        act(seq), val(seq),
                        pltpu.VMEM((seq, LANES), F32), pltpu.VMEM((seq, LANES), F32), pltpu.VMEM((seq, LANES), F32),
                        pltpu.VMEM((G_A, seq, LANES), F32), pltpu.VMEM((G_A, seq, LANES), F32),
                        pltpu.VMEM((2, seq, LANES), F32), pltpu.VMEM((2, seq, LANES), F32)],
        compiler_params=_cparams(("parallel", "parallel")),
        name="attn_prompt",
    )(qa, qa, qa, ka, va, *pair_bias)


def _attn_sample_kernel(q_ref, kt_ref, vt_ref, knt_ref, vnt_ref, bc_ref, bn_ref, o_ref, *, wbuf, t_new):
    rows = G_A * t_new

    def allowed(shape, key0):
        rr = lax.broadcasted_iota(jnp.int32, shape, 0)
        col = lax.broadcasted_iota(jnp.int32, shape, 1)
        grp = rr >> (t_new.bit_length() - 1)
        dist = wbuf + (rr - grp * t_new) - (col + key0)
        dm1 = jnp.where(grp == 0, PATTERNS[0][1] - 1, jnp.where(grp == 1, PATTERNS[1][1] - 1, PATTERNS[2][1] - 1))
        win = jnp.where(grp == 0, PATTERNS[0][0], jnp.where(grp == 1, PATTERNS[1][0], PATTERNS[2][0]))
        return (dist >= 0) & ((dist & dm1) == 0) & (dist <= win)

    ok_c = allowed((rows, wbuf), 0)
    ok_n = allowed((rows, t_new), wbuf)
    for h in range(H_A):
        qh = q_ref[0, h].astype(BF16)
        sc = jnp.where(ok_c, _dot(qh, kt_ref[0, 0, h].astype(BF16)) + bc_ref[h], NEG)
        sn = jnp.where(ok_n, _dot(qh, knt_ref[0, h].astype(BF16)) + bn_ref[h], NEG)
        mx = jnp.maximum(jnp.max(sc, axis=-1, keepdims=True), jnp.max(sn, axis=-1, keepdims=True))
        mt = jnp.maximum(jnp.maximum(mx[0:t_new], mx[t_new:2 * t_new]), mx[2 * t_new:3 * t_new])
        mall = jnp.concatenate([mt, mt, mt], axis=0)
        pc = jnp.exp(sc - mall)
        pn = jnp.exp(sn - mall)
        den = jnp.sum(pc, axis=-1, keepdims=True) + jnp.sum(pn, axis=-1, keepdims=True)
        acc = (_dot_nt(pc.astype(BF16), vt_ref[0, 0, h].astype(BF16))
               + _dot_nt(pn.astype(BF16), vnt_ref[0, h].astype(BF16)))
        dt = den[0:t_new] + den[t_new:2 * t_new] + den[2 * t_new:3 * t_new]
        at = acc[0:t_new] + acc[t_new:2 * t_new] + acc[2 * t_new:3 * t_new]
        o_ref[0, h] = at / dt


def _attn_sample(q, kt_cache, vt_cache, layer, knt, vnt, bias_c, bias_n):
    batch, wbuf = kt_cache.shape[1], kt_cache.shape[4]
    t_new = knt.shape[3]
    rows = G_A * t_new
    b4 = lambda b: (b, 0, 0, 0)
    cache_spec = pl.BlockSpec((1, 1, H_A, HD_A, wbuf), lambda b: (layer, b, 0, 0, 0))
    new_spec = pl.BlockSpec((1, H_A, HD_A, t_new), b4)
    return pl.pallas_call(
        functools.partial(_attn_sample_kernel, wbuf=wbuf, t_new=t_new),
        grid=(batch,),
        in_specs=[pl.BlockSpec((1, H_A, rows, HD_A), b4), cache_spec, cache_spec, new_spec, new_spec,
                  _resident((H_A, rows, wbuf)), _resident((H_A, rows, t_new))],
        out_specs=pl.BlockSpec((1, H_A, t_new, HD_A), b4),
        out_shape=jax.ShapeDtypeStruct((batch, H_A, t_new, HD_A), F32),
        compiler_params=_cparams(("parallel",)),
        name="attn_sample",
    )(q, kt_cache, vt_cache, knt, vnt, bias_c, bias_n)


def _cache_update_kernel(kt_ref, vt_ref, kn_ref, vn_ref, ko_ref, vo_ref, *, wbuf, t_new):
    is_new = lax.broadcasted_iota(jnp.int32, (HD_A, LANES), 1) >= LANES - t_new
    for src, new, dst in ((kt_ref, kn_ref, ko_ref), (vt_ref, vn_ref, vo_ref)):
        for h in range(H_A):
            moved = pltpu.roll(src[0, 0, h], wbuf - t_new, 1)
            dst[0, 0, h, :, 0:wbuf - LANES] = moved[:, 0:wbuf - LANES]
            dst[0, 0, h, :, wbuf - LANES:wbuf] = jnp.where(is_new, new[0, 0, h], moved[:, wbuf - LANES:wbuf])


def _cache_update(kt_cache, vt_cache, kn_tail, vn_tail, t_new):
    depth, batch, _, _, wbuf = kt_cache.shape
    idx = lambda l, b: (l, b, 0, 0, 0)
    cache_spec = pl.BlockSpec((1, 1, H_A, HD_A, wbuf), idx)
    tail_spec = pl.BlockSpec((1, 1, H_A, HD_A, LANES), idx)
    shape = jax.ShapeDtypeStruct(kt_cache.shape, F32)
    return pl.pallas_call(
        functools.partial(_cache_update_kernel, wbuf=wbuf, t_new=t_new),
        grid=(depth, batch),
        in_specs=[cache_spec, cache_spec, tail_spec, tail_spec],
        out_specs=[cache_spec, cache_spec],
        out_shape=[shape, shape],
        compiler_params=_cparams(("parallel", "parallel")),
        name="cache_update",
    )(kt_cache, vt_cache, kn_tail, vn_tail)


S5_KB = 128
S5_NBLK = D_B // S5_KB
S5_SB = D_STATE // S5_NBLK
S5_TC = 64


def _s5_kernel(u_ref, x0r_ref, x0i_ref, ar_ref, ai_ref, bre_ref, bim_ref, cre_ref, cim_ref, d_ref, wglu_ref,
               yb_ref, xr_out, xi_out, ut, ybt, bur, bui, sr, si, *, nb, tc):
    @pl.when(pl.program_id(0) == 0)
    def _():
        sr[...] = x0r_ref[...]
        si[...] = x0i_ref[...]

    for b in range(nb):
        for c in range(D_B // LANES):
            ut[c, pl.ds(b, tc, stride=nb), :] = u_ref[b, :, c * LANES:(c + 1) * LANES]
    u = jnp.concatenate([ut[c] for c in range(D_B // LANES)], axis=1)
    ub = u.astype(BF16)
    for j in range(S5_NBLK):
        uj = ub[:, j * S5_KB:(j + 1) * S5_KB]
        bur[:, j * S5_SB:(j + 1) * S5_SB] = _dot(uj, bre_ref[j])
        bui[:, j * S5_SB:(j + 1) * S5_SB] = _dot(uj, bim_ref[j])

    ar = ar_ref[...]
    ai = ai_ref[...]

    def step(t, carry):
        xr, xi = carry
        rows = pl.ds(pl.multiple_of(t * nb, nb), nb)
        nr = ar * xr - ai * xi + bur[rows, :]
        ni = ar * xi + ai * xr + bui[rows, :]
        bur[rows, :] = nr
        bui[rows, :] = ni
        return nr, ni

    xr, xi = lax.fori_loop(0, tc, step, (sr[...], si[...]))
    sr[...] = xr
    si[...] = xi
    xr_out[...] = xr
    xi_out[...] = xi

    ys = []
    for j in range(S5_NBLK):
        cs = slice(j * S5_SB, (j + 1) * S5_SB)
        ys.append(_dot(bur[:, cs].astype(BF16), cre_ref[j]) - _dot(bui[:, cs].astype(BF16), cim_ref[j]))
    y = jnp.concatenate(ys, axis=-1) + d_ref[...] * u
    ab = _dot(jax.nn.gelu(y).astype(BF16), wglu_ref[...])
    for c in range(D_MODEL // LANES):
        cs = slice(c * LANES, (c + 1) * LANES)
        ybt[c] = ab[:, cs] * jax.nn.sigmoid(ab[:, D_MODEL + c * LANES:D_MODEL + (c + 1) * LANES])
    for b in range(nb):
        for c in range(D_MODEL // LANES):
            yb_ref[b, :, c * LANES:(c + 1) * LANES] = ybt[c, pl.ds(b, tc, stride=nb), :]


def _s5(u, x0r, x0i, prm, wglu, tc):
    nb, seq, _ = u.shape
    r = tc * nb
    blk = lambda i: (0, i, 0)
    st = pl.BlockSpec((nb, D_STATE), lambda i: (0, 0))
    return pl.pallas_call(
        functools.partial(_s5_kernel, nb=nb, tc=tc),
        grid=(seq // tc,),
        in_specs=[pl.BlockSpec((nb, tc, D_B), blk), st, st,
                  _resident((1, D_STATE)), _resident((1, D_STATE)),
                  _resident((S5_NBLK, S5_KB, S5_SB)), _resident((S5_NBLK, S5_KB, S5_SB)),
                  _resident((S5_NBLK, S5_SB, S5_KB)), _resident((S5_NBLK, S5_SB, S5_KB)),
                  _resident((1, D_B)), _resident((D_B, 2 * D_MODEL))],
        out_specs=[pl.BlockSpec((nb, tc, D_MODEL), blk), st, st],
        out_shape=[jax.ShapeDtypeStruct((nb, seq, D_MODEL), F32),
                   jax.ShapeDtypeStruct((nb, D_STATE), F32), jax.ShapeDtypeStruct((nb, D_STATE), F32)],
        scratch_shapes=[pltpu.VMEM((D_B // LANES, r, LANES), F32), pltpu.VMEM((D_MODEL // LANES, r, LANES), F32),
                        pltpu.VMEM((r, D_STATE), F32), pltpu.VMEM((r, D_STATE), F32),
                        pltpu.VMEM((nb, D_STATE), F32), pltpu.VMEM((nb, D_STATE), F32)],
        compiler_params=_cparams(("arbitrary",)),
        name="s5",
    )(u, x0r, x0i, prm["ar"], prm["ai"], prm["bre"], prm["bim"], prm["cre"], prm["cim"], prm["d"], wglu)


def _s5_params(log_dt, a_re, a_im, b_re, b_im, c_re, c_im, d_skip):
    dt = jnp.exp(log_dt)[:, None]
    mag = jnp.exp(a_re * dt)
    ab_re = mag * jnp.cos(a_im * dt)
    ab_im = mag * jnp.sin(a_im * dt)
    den = a_re * a_re + a_im * a_im
    nr = ab_re - 1.0
    co_re = (nr * a_re + ab_im * a_im) / den
    co_im = (ab_im * a_re - nr * a_im) / den
    bb_re = co_re[..., None] * b_re - co_im[..., None] * b_im
    bb_im = co_re[..., None] * b_im + co_im[..., None] * b_re
    gpb = G_B // S5_NBLK
    eye = jnp.eye(gpb, dtype=F32)

    def pack_b(bb):
        t = bb.reshape(S5_NBLK, gpb, N_B, P_B)
        m = jnp.einsum('jgnp,gh->jgphn', t, eye)
        return m.reshape(S5_NBLK, gpb * P_B, gpb * N_B).astype(BF16)

    def pack_c(cc):
        t = cc.reshape(S5_NBLK, gpb, P_B, N_B)
        m = jnp.einsum('jgpn,gh->jgnhp', t, eye)
        return m.reshape(S5_NBLK, gpb * N_B, gpb * P_B).astype(BF16)

    return dict(ar=ab_re.reshape(1, D_STATE), ai=ab_im.reshape(1, D_STATE),
                bre=pack_b(bb_re), bim=pack_b(bb_im), cre=pack_c(c_re), cim=pack_c(c_im),
                d=d_skip.reshape(1, D_B))


def _gla_kernel(q_ref, k_ref, v_ref, r_ref, fc_ref, wfg_ref, bfg_ref, gn_ref, s0_ref, y_ref, sfin_ref, st,
                *, chunk, nchunk):
    tb = pl.program_id(1)

    @pl.when(tb == 0)
    def _():
        for h in range(H_C):
            st[h] = s0_ref[0, h].T

    rows_blk = chunk * nchunk
    la = jax.nn.log_sigmoid(_dot(fc_ref[...].astype(BF16), wfg_ref[...]) + bfg_ref[...]) / GLA_TAU
    ri = lax.broadcasted_iota(jnp.int32, (rows_blk, rows_blk), 0)
    ci = lax.broadcasted_iota(jnp.int32, (rows_blk, rows_blk), 1)
    sh = chunk.bit_length() - 1
    same = (ri >> sh) == (ci >> sh)
    tri = jnp.where(same & (ci <= ri), 1.0, 0.0).astype(BF16)
    la_hi = la.astype(BF16)
    la_lo = (la - la_hi.astype(F32)).astype(BF16)
    bcum = _dot(tri, la_hi) + _dot(tri, la_lo)
    causal = lax.broadcasted_iota(jnp.int32, (chunk, chunk), 1) <= lax.broadcasted_iota(jnp.int32, (chunk, chunk), 0)
    gn = gn_ref[...]
    for c in range(nchunk):
        rs = slice(c * chunk, (c + 1) * chunk)
        b = bcum[rs]
        bl = b[chunk - 1:chunk]
        qt = (q_ref[rs, :] * jnp.exp(b)).astype(BF16)
        kt = (k_ref[rs, :] * jnp.exp(-b)).astype(BF16)
        kd = (k_ref[rs, :] * jnp.exp(bl - b)).astype(BF16)
        dec = jnp.exp(bl)
        vb = v_ref[rs, :].astype(BF16)
        for h in range(H_C):
            ks = slice(h * DK_C, (h + 1) * DK_C)
            vs = slice(h * DV_C, (h + 1) * DV_C)
            s_t = st[h]
            att = jnp.where(causal, _dot_nt(qt[:, ks], kt[:, ks]), 0.0)
            o = _dot(att.astype(BF16), vb[:, vs]) + _dot_nt(qt[:, ks], s_t.astype(BF16))
            st[h] = s_t * dec[:, ks] + _dot_tn(vb[:, vs], kd[:, ks])
            y_ref[rs, vs] = _rms(o, gn) * jax.nn.silu(r_ref[rs, vs])

    @pl.when(tb == pl.num_programs(1) - 1)
    def _():
        for h in range(H_C):
            sfin_ref[0, h] = st[h].T


def _gla(qc, kc, vc, rc, fc, wfg, bfg, gnorm, s0, batch, seq):
    chunk = math.gcd(seq, GLA_CHUNK)
    nchunk = min(seq // chunk, 4)
    rows_blk = chunk * nchunk
    nt = seq // rows_blk
    row = lambda b, t: (b * nt + t, 0)
    st_spec = pl.BlockSpec((1, H_C, DK_C, DV_C), lambda b, t: (b, 0, 0, 0))
    return pl.pallas_call(
        functools.partial(_gla_kernel, chunk=chunk, nchunk=nchunk),
        grid=(batch, nt),
        in_specs=[pl.BlockSpec((rows_blk, H_C * DK_C), row), pl.BlockSpec((rows_blk, H_C * DK_C), row),
                  pl.BlockSpec((rows_blk, H_C * DV_C), row), pl.BlockSpec((rows_blk, H_C * DV_C), row),
                  pl.BlockSpec((rows_blk, FG_PAD), row),
                  _resident((FG_PAD, H_C * DK_C)), _resident((1, H_C * DK_C)), _resident((1, DV_C)), st_spec],
        out_specs=[pl.BlockSpec((rows_blk, H_C * DV_C), row), st_spec],
        out_shape=[jax.ShapeDtypeStruct((batch * seq, H_C * DV_C), F32),
                   jax.ShapeDtypeStruct((batch, H_C, DK_C, DV_C), F32)],
        scratch_shapes=[pltpu.VMEM((H_C, DV_C, DK_C), F32)],
        compiler_params=_cparams(("parallel", "arbitrary")),
        name="gla",
    )(qc, kc, vc, rc, fc, wfg, bfg, gnorm, s0)


def _post_kernel(x_ref, oa_ref, gates_ref, yb_ref, yc_ref, woa_ref, woc_ref, wout_ref, gmlp_ref, wup_ref, wdn_ref,
                 gfin_ref, out_ref, *, final):
    y_a = _dot(oa_ref[...].astype(BF16), woa_ref[...])
    y_c = _dot(yc_ref[...].astype(BF16), woc_ref[...])
    gates = gates_ref[...]
    m = (jax.nn.sigmoid(gates[:, :D_MODEL]) * y_a + jax.nn.sigmoid(gates[:, D_MODEL:2 * D_MODEL]) * yb_ref[...]
         + jax.nn.sigmoid(gates[:, 2 * D_MODEL:]) * y_c)
    x1 = x_ref[...] + _dot(m.astype(BF16), wout_ref[...])
    hm = _rms(x1, gmlp_ref[...]).astype(BF16)
    up = _dot(hm, wup_ref[...])
    act = jnp.square(jnp.maximum(up, 0.0)).astype(BF16)
    x2 = x1 + _dot(act, wdn_ref[...])
    out_ref[...] = _rms(x2, gfin_ref[...]) if final else x2


def _post(x, oa, gates, yb, yc, lw, gfin, final, tm):
    m = x.shape[0]
    row = lambda i: (i, 0)
    tok = lambda wd: pl.BlockSpec((tm, wd), row)
    return pl.pallas_call(
        functools.partial(_post_kernel, final=final),
        grid=(m // tm,),
        in_specs=[tok(D_MODEL), tok(D_A), tok(3 * D_MODEL), tok(D_MODEL), tok(H_C * DV_C),
                  _resident((D_A, D_MODEL)), _resident((H_C * DV_C, D_MODEL)), _resident((D_MODEL, D_MODEL)),
                  _resident((1, D_MODEL)), _resident((D_MODEL, D_FF)), _resident((D_FF, D_MODEL)),
                  _resident((1, D_MODEL))],
        out_specs=tok(D_MODEL),
        out_shape=jax.ShapeDtypeStruct((m, D_MODEL), F32),
        compiler_params=_cparams(("parallel",)),
        name="post",
    )(x, oa, gates, yb, yc, lw["woa"], lw["woc"], lw["wout"], lw["gmlp"], lw["wup"], lw["wdn"], gfin)


def _t5_bucket(dist):
    exact = N_BUCKETS // 2
    d = np.maximum(dist, 1).astype(np.float32)
    large = exact + (np.log(d / exact) / np.log(W_MAX / exact) * (N_BUCKETS - exact)).astype(np.int32)
    large = np.minimum(large, N_BUCKETS - 1)
    return np.where(dist < exact, dist, large).astype(np.int32)


def _toeplitz(u, rows, cols, off):
    period = cols + off + 1
    w = jnp.pad(u, ((0, 0), (0, period - u.shape[1])))
    a = jnp.tile(w, (1, rows))[:, :rows * (period - 1)].reshape(u.shape[0], rows, period - 1)
    return a[:, :, off:off + cols]


def _prompt_bias(rel_bias, g, dil):
    x = np.arange(3 * N_BACK - 1)
    bk = _t5_bucket(np.clip(2 * N_BACK - 1 - x, 0, N_BACK) * dil)
    u = rel_bias[bk][:, g * H_A:(g + 1) * H_A].T
    return _toeplitz(u, N_BACK, 2 * N_BACK, N_BACK - 1)


def _sample_bias(rel_bias, wbuf, t_new):
    n = wbuf + t_new
    x = np.arange(n + t_new - 1)
    bk = _t5_bucket(np.clip(n - 1 - x, 0, W_MAX))
    tab = rel_bias[bk]
    per_g = [_toeplitz(tab[:, g * H_A:(g + 1) * H_A].T, t_new, n, t_new - 1) for g in range(G_A)]
    full = jnp.concatenate(per_g, axis=1)
    return full[:, :, :wbuf], full[:, :, wbuf:]


def _layer_weights(l, w_in, w_o_a, w_glu, w_fg2, b_fg, gla_norm, w_o_c, w_out, norm_mix, norm_mlp, w_up, w_down):
    wl = w_in[l]
    cuts = np.cumsum((0, G_A * D_A, D_A, D_A, D_B, H_C * DK_C, H_C * DK_C, H_C * DV_C, H_C * DV_C, FG_RANK,
                      3 * D_MODEL))
    segs = [wl[:, cuts[i]:cuts[i + 1]] for i in range(len(cuts) - 1)]
    segs[8] = jnp.pad(segs[8], ((0, 0), (0, FG_PAD - FG_RANK)))
    return dict(
        win=jnp.concatenate(segs, axis=1).astype(BF16),
        gmix=norm_mix[l].reshape(1, D_MODEL),
        wglu=w_glu[l].astype(BF16),
        wfg=jnp.pad(w_fg2[l], ((0, FG_PAD - FG_RANK), (0, 0))).astype(BF16),
        bfg=b_fg[l].reshape(1, H_C * DK_C),
        gnorm=gla_norm[l].reshape(1, DV_C),
        woa=w_o_a[l].astype(BF16), woc=w_o_c[l].astype(BF16), wout=w_out[l].astype(BF16),
        gmlp=norm_mlp[l].reshape(1, D_MODEL),
        wup=w_up[l].astype(BF16), wdn=w_down[l].astype(BF16),
    )


def kernel(x_prompt, x_sample, cache_k_win, cache_v_win, state_ssm_re, state_ssm_im, state_gla, rel_bias, norm_mix, w_in, w_o_a, s5_log_dt, s5_a_re, s5_a_im, s5_b_re, s5_b_im, s5_c_re, s5_c_im, s5_d, w_glu, w_fg2, b_fg, gla_norm, w_o_c, w_out, norm_mlp, w_up, w_down, norm_final):
    bp, sp, _ = x_prompt.shape
    bs, ts, _ = x_sample.shape
    wbuf = cache_k_win.shape[2]
    mp, ms = bp * sp, bs * ts
    gfin = norm_final.reshape(1, D_MODEL)
    kt_cache = jnp.transpose(cache_k_win, (0, 1, 3, 4, 2))
    vt_cache = jnp.transpose(cache_v_win, (0, 1, 3, 4, 2))
    pbias = [_prompt_bias(rel_bias, g, dil) for g, (_, dil) in enumerate(PATTERNS)]
    sbias_c, sbias_n = _sample_bias(rel_bias, wbuf, ts)

    xp = x_prompt.reshape(mp, D_MODEL)
    xs = x_sample.reshape(ms, D_MODEL)
    outs = {k: [] for k in ("kp", "vp", "kn", "vn", "rp", "ip", "rs", "is", "gp", "gs")}
    for l in range(DEPTH):
        lw = _layer_weights(l, w_in, w_o_a, w_glu, w_fg2, b_fg, gla_norm, w_o_c, w_out, norm_mix, norm_mlp,
                            w_up, w_down)
        s5p = _s5_params(s5_log_dt[l], s5_a_re[l], s5_a_im[l], s5_b_re[l], s5_b_im[l], s5_c_re[l], s5_c_im[l],
                         s5_d[l])
        final = l == DEPTH - 1

        qa, ka, va, ub, qc, kc, vc, rc, fc, gates = _in_proj(xp, lw["gmix"], lw["win"], 256)
        oa = _attn_prompt(qa, ka, va, pbias, bp, sp)
        zeros = jnp.zeros((bp, D_STATE), F32)
        yb, xr, xi = _s5(ub.reshape(bp, sp, D_B), zeros, zeros, s5p, lw["wglu"], S5_TC)
        yb = yb.reshape(mp, D_MODEL)
        yc, gst = _gla(qc, kc, vc, rc, fc, lw["wfg"], lw["bfg"], lw["gnorm"],
                       jnp.zeros((bp, H_C, DK_C, DV_C), F32), bp, sp)
        xp = _post(xp, oa, gates, yb, yc, lw, gfin, final, 256)
        outs["kp"].append(ka.reshape(bp, sp, H_A, HD_A)[:, -min(W_MAX, sp):])
        outs["vp"].append(va.reshape(bp, sp, H_A, HD_A)[:, -min(W_MAX, sp):])
        outs["rp"].append(xr.reshape(bp, G_B, N_B))
        outs["ip"].append(xi.reshape(bp, G_B, N_B))
        outs["gp"].append(gst)

        qa, ka, va, ub, qc, kc, vc, rc, fc, gates = _in_proj(xs, lw["gmix"], lw["win"], 256)
        q4 = qa.reshape(bs, ts, G_A, H_A, HD_A).transpose(0, 3, 2, 1, 4).reshape(bs, H_A, G_A * ts, HD_A)
        knt = ka.reshape(bs, ts, H_A, HD_A).transpose(0, 2, 3, 1)
        vnt = va.reshape(bs, ts, H_A, HD_A).transpose(0, 2, 3, 1)
        o4 = _attn_sample(q4, kt_cache, vt_cache, l, knt, vnt, sbias_c, sbias_n)
        oa = o4.transpose(0, 2, 1, 3).reshape(ms, D_A)
        yb, xr, xi = _s5(ub.reshape(bs, ts, D_B), state_ssm_re[l].reshape(bs, D_STATE),
                         state_ssm_im[l].reshape(bs, D_STATE), s5p, lw["wglu"], ts)
        yb = yb.reshape(ms, D_MODEL)
        yc, gst = _gla(qc, kc, vc, rc, fc, lw["wfg"], lw["bfg"], lw["gnorm"], state_gla[l], bs, ts)
        xs = _post(xs, oa, gates, yb, yc, lw, gfin, final, 256)
        outs["kn"].append(knt)
        outs["vn"].append(vnt)
        outs["rs"].append(xr.reshape(bs, G_B, N_B))
        outs["is"].append(xi.reshape(bs, G_B, N_B))
        outs["gs"].append(gst)

    st = {k: jnp.stack(v) for k, v in outs.items()}
    tail = ((0, 0), (0, 0), (0, 0), (0, 0), (LANES - ts, 0))
    kt_new, vt_new = _cache_update(kt_cache, vt_cache, jnp.pad(st["kn"], tail), jnp.pad(st["vn"], tail), ts)
    ks = jnp.transpose(kt_new, (0, 1, 4, 2, 3))
    vs = jnp.transpose(vt_new, (0, 1, 4, 2, 3))
    return (xp.reshape(bp, sp, D_MODEL), xs.reshape(bs, ts, D_MODEL),
            st["kp"], st["vp"], ks, vs, st["rp"], st["ip"], st["rs"], st["is"], st["gp"], st["gs"])
```

```python
import functools
import math

import numpy as np
import jax
import jax.numpy as jnp
from jax import lax
from jax.experimental import pallas as pl
from jax.experimental.pallas import tpu as pltpu

F32 = jnp.float32
BF16 = jnp.bfloat16

LANES = 128
D_MODEL = 1024
DEPTH = 4
H_A = 8
HD_A = 64
D_A = H_A * HD_A
N_PAIR = D_A // LANES
PATTERNS = ((128, 1), (512, 4), (2048, 16))
G_A = len(PATTERNS)
N_BACK = 128
W_MAX = 2048
N_BUCKETS = 32
P_B = 16
D_B = 512
G_B = D_B // P_B
N_B = 64
D_STATE = G_B * N_B
H_C = 4
DK_C = 64
DV_C = 128
FG_RANK = 16
FG_PAD = 128
GLA_TAU = 16.0
GLA_CHUNK = 64
D_FF = 4 * D_MODEL
EPS = 1e-6
NEG = -1e30

SEG_WIDTHS = (G_A * D_A, D_A, D_A, D_B, H_C * DK_C, H_C * DK_C, H_C * DV_C, H_C * DV_C, FG_PAD, 3 * D_MODEL)
SEG_OFFS = tuple(int(v) for v in np.cumsum((0,) + SEG_WIDTHS))
IN_COLS_PAD = SEG_OFFS[-1]

VMEM_LIMIT = 56 * 1024 * 1024
TM = 256


def _cparams(sem):
    return pltpu.CompilerParams(dimension_semantics=sem, vmem_limit_bytes=VMEM_LIMIT)


def _resident(shape, layer=None):
    nd = len(shape)
    if layer is None:
        return pl.BlockSpec(shape, lambda *_: (0,) * nd, pipeline_mode=pl.Buffered(1))
    return pl.BlockSpec((None,) + tuple(shape), lambda *_: (layer,) + (0,) * nd, pipeline_mode=pl.Buffered(1))


def _rms(x, g):
    return x * lax.rsqrt(jnp.mean(x * x, axis=-1, keepdims=True) + EPS) * g


def _dot(a, b):
    return jnp.dot(a, b, preferred_element_type=F32)


def _dot_nt(a, b):
    return lax.dot_general(a, b, (((1,), (1,)), ((), ())), preferred_element_type=F32)


def _dot_tn(a, b):
    return lax.dot_general(a, b, (((0,), (0,)), ((), ())), preferred_element_type=F32)


def _in_proj_kernel(x_ref, g_ref, w_ref, *out_refs):
    hb = _rms(x_ref[...], g_ref[...]).astype(BF16)
    for s, o_ref in enumerate(out_refs[:len(SEG_WIDTHS)]):
        z = _dot(hb, w_ref[:, SEG_OFFS[s]:SEG_OFFS[s + 1]])
        if s in (0, 4):
            z = z * 0.125
        o_ref[...] = z
        if s in (1, 2) and len(out_refs) > len(SEG_WIDTHS):
            out_refs[len(SEG_WIDTHS) + s - 1][0] = z.T.reshape(H_A, HD_A, z.shape[0])


def _in_proj(x, g, w, layer, tm, kv_t_shape=None):
    m = x.shape[0]
    row = lambda i: (i, 0)
    out_specs = [pl.BlockSpec((tm, wd), row) for wd in SEG_WIDTHS]
    out_shape = [jax.ShapeDtypeStruct((m, wd), F32) for wd in SEG_WIDTHS]
    if kv_t_shape is not None:
        batch, seq = kv_t_shape
        per = seq // tm
        out_specs += [pl.BlockSpec((1, H_A, HD_A, tm), lambda i: (i // per, 0, 0, i % per))] * 2
        out_shape += [jax.ShapeDtypeStruct((batch, H_A, HD_A, seq), F32)] * 2
    return pl.pallas_call(
        _in_proj_kernel,
        grid=(m // tm,),
        in_specs=[pl.BlockSpec((tm, D_MODEL), row), _resident((1, D_MODEL), layer),
                  _resident((D_MODEL, IN_COLS_PAD), layer)],
        out_specs=out_specs,
        out_shape=out_shape,
        compiler_params=_cparams(("parallel",)),
        name="in_proj",
    )(x, g, w)


ATTN_UNROLL = 8


def _attn_prompt_kernel(q0_ref, q1_ref, q2_ref, k_ref, v_ref, b0_ref, b1_ref, b2_ref, o_ref,
                        qs, k1, v1, k4, v4, k16, v16, tq, tk, tv, og, lg, uo, ul, *, seq):
    nu = seq // N_BACK
    c4 = seq // 4
    left = lax.broadcasted_iota(jnp.int32, (N_BACK, LANES), 1) < HD_A

    def put_q(g, u, q):
        qs[g, 2 * N_BACK * u:2 * N_BACK * u + N_BACK, :] = jnp.where(left, q, 0.0).astype(BF16)
        qs[g, 2 * N_BACK * u + N_BACK:2 * N_BACK * (u + 1), :] = jnp.where(left, 0.0, q).astype(BF16)

    zpad = jnp.zeros((N_BACK, LANES), BF16)
    for buf in (k1, k4, v1, v4):
        buf[0:N_BACK, 0:LANES] = zpad
    for buf in (v1, v4, v16):
        buf[:, LANES:] = jnp.ones((buf.shape[0], LANES), BF16)
    k1[N_BACK:N_BACK + seq, :] = k_ref[...].astype(BF16)
    v1[N_BACK:N_BACK + seq, 0:LANES] = v_ref[...].astype(BF16)
    for r in range(4):
        by4 = pl.ds(r, c4, stride=4)
        tk[r * c4:(r + 1) * c4, :] = k_ref[by4, :]
        tv[r * c4:(r + 1) * c4, :] = v_ref[by4, :]
        tq[r * c4:(r + 1) * c4, :] = q2_ref[by4, :]
    k4[N_BACK:N_BACK + seq, :] = tk[...].astype(BF16)
    v4[N_BACK:N_BACK + seq, 0:LANES] = tv[...].astype(BF16)
    for u in range(nu):
        put_q(0, u, q0_ref[u * N_BACK:(u + 1) * N_BACK, :])
        put_q(1, u, q1_ref[pl.ds(u // 4 + 4 * N_BACK * (u % 4), N_BACK, stride=4), :])
        by16 = pl.ds((u % 4) * c4 + u // 4, N_BACK, stride=4)
        put_q(2, u, tq[by16, :])
        k16[u * N_BACK:(u + 1) * N_BACK, :] = tk[by16, :].astype(BF16)
        v16[u * N_BACK:(u + 1) * N_BACK, 0:LANES] = tv[by16, :].astype(BF16)

    shp = (2 * N_BACK, 2 * N_BACK)
    qi = lax.broadcasted_iota(jnp.int32, shp, 0) & (N_BACK - 1)
    kk = lax.broadcasted_iota(jnp.int32, shp, 1)
    step = N_BACK + qi - kk
    band = (step >= 0) & (step <= N_BACK)
    newer = kk >= N_BACK
    tri = band[:, N_BACK:]

    def unit(u, carry, *, g, kbuf, vbuf, b_ref, class_blocks):
        q = qs[g, pl.ds(pl.multiple_of(u * 2 * N_BACK, 2 * N_BACK), 2 * N_BACK), :]
        k0 = pl.multiple_of(u * N_BACK, N_BACK)
        if class_blocks > 1:
            krows = pl.ds(k0, 2 * N_BACK)
            valid = band & (newer | ((u & (class_blocks - 1)) > 0))
            bias = b_ref[0]
        else:
            krows = pl.ds(k0, N_BACK)
            valid = tri
            bias = b_ref[0, :, N_BACK:]
        s = jnp.where(valid, _dot_nt(q, kbuf[krows, :]) + bias, NEG)
        mx = jnp.max(s, axis=-1, keepdims=True)
        p = jnp.exp(s - mx)
        pv = _dot(p.astype(BF16), vbuf[krows, :])
        den = pv[:, LANES:]
        o = pv[:, 0:LANES] / den
        lse = mx + jnp.log(den)
        og[g, pl.ds(k0, N_BACK), :] = jnp.where(left, o[0:N_BACK], o[N_BACK:])
        lg[g, pl.ds(k0, N_BACK), :] = jnp.where(left, lse[0:N_BACK], lse[N_BACK:])
        return carry

    for g, (kbuf, vbuf, b_ref, class_blocks) in enumerate(((k1, v1, b0_ref, nu), (k4, v4, b1_ref, nu // 4),
                                                           (k16, v16, b2_ref, 1))):
        lax.fori_loop(0, nu, functools.partial(unit, g=g, kbuf=kbuf, vbuf=vbuf, b_ref=b_ref,
                                               class_blocks=class_blocks), 0, unroll=ATTN_UNROLL)

    for r in range(4):
        by4 = pl.ds(r, c4, stride=4)
        uo[0, by4, :] = og[1, r * c4:(r + 1) * c4, :]
        ul[0, by4, :] = lg[1, r * c4:(r + 1) * c4, :]
    for u in range(nu):
        by16 = pl.ds((u % 4) * c4 + u // 4, N_BACK, stride=4)
        tq[by16, :] = og[2, u * N_BACK:(u + 1) * N_BACK, :]
        tk[by16, :] = lg[2, u * N_BACK:(u + 1) * N_BACK, :]
    for r in range(4):
        by4 = pl.ds(r, c4, stride=4)
        uo[1, by4, :] = tq[r * c4:(r + 1) * c4, :]
        ul[1, by4, :] = tk[r * c4:(r + 1) * c4, :]

    def merge(i, carry):
        rows = pl.ds(pl.multiple_of(i * N_BACK, N_BACK), N_BACK)
        ls = [lg[0, rows, :], ul[0, rows, :], ul[1, rows, :]]
        mx = jnp.maximum(jnp.maximum(ls[0], ls[1]), ls[2])
        ws = [jnp.exp(l - mx) for l in ls]
        num = ws[0] * og[0, rows, :] + ws[1] * uo[0, rows, :] + ws[2] * uo[1, rows, :]
        o_ref[rows, :] = num / (ws[0] + ws[1] + ws[2])
        return carry

    lax.fori_loop(0, nu, merge, 0)


def _attn_prompt(qa, ka, va, biasmats, batch, seq):
    assert tuple(d for _, d in PATTERNS) == (1, 4, 16) and seq == 16 * N_BACK
    blk = (seq, LANES)
    qspec = lambda g: pl.BlockSpec(blk, lambda b, p: (b, g * N_PAIR + p))
    kspec = pl.BlockSpec(blk, lambda b, p: (b, p))
    bspec = pl.BlockSpec((1, 2 * N_BACK, 2 * N_BACK), lambda b, p: (p, 0, 0))
    pair_bias = [bm.reshape(N_PAIR, 2 * N_BACK, 2 * N_BACK) for bm in biasmats]
    act = lambda rows: pltpu.VMEM((rows, LANES), BF16)
    val = lambda rows: pltpu.VMEM((rows, 2 * LANES), BF16)
    return pl.pallas_call(
        functools.partial(_attn_prompt_kernel, seq=seq),
        grid=(batch, N_PAIR),
        in_specs=[qspec(0), qspec(1), qspec(2), kspec, kspec, bspec, bspec, bspec],
        out_specs=kspec,
        out_shape=jax.ShapeDtypeStruct((batch * seq, D_A), F32),
        scratch_shapes=[pltpu.VMEM((G_A, 2 * seq, LANES), BF16),
                        act(N_BACK + seq), val(N_BACK + seq), act(N_BACK + seq), val(N_BACK + seq),
                        act(seq), val(seq),
                        pltpu.VMEM((seq, LANES), F32), pltpu.VMEM((seq, LANES), F32), pltpu.VMEM((seq, LANES), F32),
                        pltpu.VMEM((G_A, seq, LANES), F32), pltpu.VMEM((G_A, seq, LANES), F32),
                        pltpu.VMEM((2, seq, LANES), F32), pltpu.VMEM((2, seq, LANES), F32)],
        compiler_params=_cparams(("parallel", "parallel")),
        name="attn_prompt",
    )(qa, qa, qa, ka, va, *pair_bias)


def _attn_sample_kernel(q_ref, kt_ref, vt_ref, knt_ref, vnt_ref, bc_ref, bn_ref, o_ref, *, wbuf, t_new):
    rows = G_A * t_new

    def allowed(shape, key0):
        rr = lax.broadcasted_iota(jnp.int32, shape, 0)
        col = lax.broadcasted_iota(jnp.int32, shape, 1)
        grp = rr >> (t_new.bit_length() - 1)
        dist = wbuf + (rr - grp * t_new) - (col + key0)
        dm1 = jnp.where(grp == 0, PATTERNS[0][1] - 1, jnp.where(grp == 1, PATTERNS[1][1] - 1, PATTERNS[2][1] - 1))
        win = jnp.where(grp == 0, PATTERNS[0][0], jnp.where(grp == 1, PATTERNS[1][0], PATTERNS[2][0]))
        return (dist >= 0) & ((dist & dm1) == 0) & (dist <= win)

    ok_c = allowed((rows, wbuf), 0)
    ok_n = allowed((rows, t_new), wbuf)
    for h in range(H_A):
        qh = q_ref[0, h].astype(BF16)
        sc = jnp.where(ok_c, _dot(qh, kt_ref[0, 0, h].astype(BF16)) + bc_ref[h], NEG)
        sn = jnp.where(ok_n, _dot(qh, knt_ref[0, h].astype(BF16)) + bn_ref[h], NEG)
        mx = jnp.maximum(jnp.max(sc, axis=-1, keepdims=True), jnp.max(sn, axis=-1, keepdims=True))
        mt = jnp.maximum(jnp.maximum(mx[0:t_new], mx[t_new:2 * t_new]), mx[2 * t_new:3 * t_new])
        mall = jnp.concatenate([mt, mt, mt], axis=0)
        pc = jnp.exp(sc - mall)
        pn = jnp.exp(sn - mall)
        den = jnp.sum(pc, axis=-1, keepdims=True) + jnp.sum(pn, axis=-1, keepdims=True)
        acc = (_dot_nt(pc.astype(BF16), vt_ref[0, 0, h].astype(BF16))
               + _dot_nt(pn.astype(BF16), vnt_ref[0, h].astype(BF16)))
        dt = den[0:t_new] + den[t_new:2 * t_new] + den[2 * t_new:3 * t_new]
        at = acc[0:t_new] + acc[t_new:2 * t_new] + acc[2 * t_new:3 * t_new]
        o_ref[0, h] = at / dt


def _attn_sample(q, kt_cache, vt_cache, layer, knt, vnt, bias_c, bias_n):
    batch, wbuf = kt_cache.shape[1], kt_cache.shape[4]
    t_new = knt.shape[3]
    rows = G_A * t_new
    b4 = lambda b: (b, 0, 0, 0)
    cache_spec = pl.BlockSpec((1, 1, H_A, HD_A, wbuf), lambda b: (layer, b, 0, 0, 0))
    new_spec = pl.BlockSpec((1, H_A, HD_A, t_new), b4)
    return pl.pallas_call(
        functools.partial(_attn_sample_kernel, wbuf=wbuf, t_new=t_new),
        grid=(batch,),
        in_specs=[pl.BlockSpec((1, H_A, rows, HD_A), b4), cache_spec, cache_spec, new_spec, new_spec,
                  _resident((H_A, rows, wbuf)), _resident((H_A, rows, t_new))],
        out_specs=pl.BlockSpec((1, H_A, t_new, HD_A), b4),
        out_shape=jax.ShapeDtypeStruct((batch, H_A, t_new, HD_A), F32),
        compiler_params=_cparams(("parallel",)),
        name="attn_sample",
    )(q, kt_cache, vt_cache, knt, vnt, bias_c, bias_n)


def _cache_update_kernel(kt_ref, vt_ref, kn_ref, vn_ref, ko_ref, vo_ref, *, wbuf, t_new):
    is_new = lax.broadcasted_iota(jnp.int32, (HD_A, LANES), 1) >= LANES - t_new
    for src, new, dst in ((kt_ref, kn_ref, ko_ref), (vt_ref, vn_ref, vo_ref)):
        for h in range(H_A):
            moved = pltpu.roll(src[0, 0, h], wbuf - t_new, 1)
            dst[0, 0, h, :, 0:wbuf - LANES] = moved[:, 0:wbuf - LANES]
            dst[0, 0, h, :, wbuf - LANES:wbuf] = jnp.where(is_new, new[0, 0, h], moved[:, wbuf - LANES:wbuf])


def _cache_update(kt_cache, vt_cache, kn_tail, vn_tail, t_new):
    depth, batch, _, _, wbuf = kt_cache.shape
    idx = lambda l, b: (l, b, 0, 0, 0)
    cache_spec = pl.BlockSpec((1, 1, H_A, HD_A, wbuf), idx)
    tail_spec = pl.BlockSpec((1, 1, H_A, HD_A, LANES), idx)
    shape = jax.ShapeDtypeStruct(kt_cache.shape, F32)
    return pl.pallas_call(
        functools.partial(_cache_update_kernel, wbuf=wbuf, t_new=t_new),
        grid=(depth, batch),
        in_specs=[cache_spec, cache_spec, tail_spec, tail_spec],
        out_specs=[cache_spec, cache_spec],
        out_shape=[shape, shape],
        compiler_params=_cparams(("parallel", "parallel")),
        name="cache_update",
    )(kt_cache, vt_cache, kn_tail, vn_tail)


S5_KB = 128
S5_NBLK = D_B // S5_KB
S5_SB = D_STATE // S5_NBLK
S5_TC = 64


def _s5_kernel(u_ref, x0r_ref, x0i_ref, ar_ref, ai_ref, bre_ref, bim_ref, cre_ref, cim_ref, d_ref, wglu_ref,
               yb_ref, xr_out, xi_out, ut, ybt, bur, bui, sr, si, *, nb, tc):
    @pl.when(pl.program_id(0) == 0)
    def _():
        sr[...] = x0r_ref[...]
        si[...] = x0i_ref[...]

    for b in range(nb):
        for c in range(D_B // LANES):
            ut[c, pl.ds(b, tc, stride=nb), :] = u_ref[b, :, c * LANES:(c + 1) * LANES]
    u = jnp.concatenate([ut[c] for c in range(D_B // LANES)], axis=1)
    ub = u.astype(BF16)
    for j in range(S5_NBLK):
        uj = ub[:, j * S5_KB:(j + 1) * S5_KB]
        bur[:, j * S5_SB:(j + 1) * S5_SB] = _dot(uj, bre_ref[j])
        bui[:, j * S5_SB:(j + 1) * S5_SB] = _dot(uj, bim_ref[j])

    ar = ar_ref[...]
    ai = ai_ref[...]

    def step(t, carry):
        xr, xi = carry
        rows = pl.ds(pl.multiple_of(t * nb, nb), nb)
        nr = ar * xr - ai * xi + bur[rows, :]
        ni = ar * xi + ai * xr + bui[rows, :]
        bur[rows, :] = nr
        bui[rows, :] = ni
        return nr, ni

    xr, xi = lax.fori_loop(0, tc, step, (sr[...], si[...]))
    sr[...] = xr
    si[...] = xi
    xr_out[...] = xr
    xi_out[...] = xi

    ys = []
    for j in range(S5_NBLK):
        cs = slice(j * S5_SB, (j + 1) * S5_SB)
        ys.append(_dot(bur[:, cs].astype(BF16), cre_ref[j]) - _dot(bui[:, cs].astype(BF16), cim_ref[j]))
    y = jnp.concatenate(ys, axis=-1) + d_ref[...] * u
    ab = _dot(jax.nn.gelu(y).astype(BF16), wglu_ref[...])
    for c in range(D_MODEL // LANES):
        cs = slice(c * LANES, (c + 1) * LANES)
        ybt[c] = ab[:, cs] * jax.nn.sigmoid(ab[:, D_MODEL + c * LANES:D_MODEL + (c + 1) * LANES])
    for b in range(nb):
        for c in range(D_MODEL // LANES):
            yb_ref[b, :, c * LANES:(c + 1) * LANES] = ybt[c, pl.ds(b, tc, stride=nb), :]


def _s5(u, x0r, x0i, prm, wglu, layer, tc):
    nb, seq, _ = u.shape
    r = tc * nb
    blk = lambda i: (0, i, 0)
    st = pl.BlockSpec((nb, D_STATE), lambda i: (0, 0))
    return pl.pallas_call(
        functools.partial(_s5_kernel, nb=nb, tc=tc),
        grid=(seq // tc,),
        in_specs=[pl.BlockSpec((nb, tc, D_B), blk), st, st,
                  _resident((1, D_STATE), layer), _resident((1, D_STATE), layer),
                  _resident((S5_NBLK, S5_KB, S5_SB), layer), _resident((S5_NBLK, S5_KB, S5_SB), layer),
                  _resident((S5_NBLK, S5_SB, S5_KB), layer), _resident((S5_NBLK, S5_SB, S5_KB), layer),
                  _resident((1, D_B), layer), _resident((D_B, 2 * D_MODEL), layer)],
        out_specs=[pl.BlockSpec((nb, tc, D_MODEL), blk), st, st],
        out_shape=[jax.ShapeDtypeStruct((nb, seq, D_MODEL), F32),
                   jax.ShapeDtypeStruct((nb, D_STATE), F32), jax.ShapeDtypeStruct((nb, D_STATE), F32)],
        scratch_shapes=[pltpu.VMEM((D_B // LANES, r, LANES), F32), pltpu.VMEM((D_MODEL // LANES, r, LANES), F32),
                        pltpu.VMEM((r, D_STATE), F32), pltpu.VMEM((r, D_STATE), F32),
                        pltpu.VMEM((nb, D_STATE), F32), pltpu.VMEM((nb, D_STATE), F32)],
        compiler_params=_cparams(("arbitrary",)),
        name="s5",
    )(u, x0r, x0i, prm["ar"], prm["ai"], prm["bre"], prm["bim"], prm["cre"], prm["cim"], prm["d"], wglu)


def _s5_params(log_dt, a_re, a_im, b_re, b_im, c_re, c_im, d_skip):
    depth = log_dt.shape[0]
    dt = jnp.exp(log_dt)[..., None]
    mag = jnp.exp(a_re * dt)
    ab_re = mag * jnp.cos(a_im * dt)
    ab_im = mag * jnp.sin(a_im * dt)
    den = a_re * a_re + a_im * a_im
    nr = ab_re - 1.0
    co_re = (nr * a_re + ab_im * a_im) / den
    co_im = (ab_im * a_re - nr * a_im) / den
    bb_re = co_re[..., None] * b_re - co_im[..., None] * b_im
    bb_im = co_re[..., None] * b_im + co_im[..., None] * b_re
    gpb = G_B // S5_NBLK
    eye = jnp.eye(gpb, dtype=F32)

    def pack_b(bb):
        t = bb.reshape(depth, S5_NBLK, gpb, N_B, P_B)
        m = jnp.einsum('ljgnp,gh->ljgphn', t, eye)
        return m.reshape(depth, S5_NBLK, gpb * P_B, gpb * N_B).astype(BF16)

    def pack_c(cc):
        t = cc.reshape(depth, S5_NBLK, gpb, P_B, N_B)
        m = jnp.einsum('ljgpn,gh->ljgnhp', t, eye)
        return m.reshape(depth, S5_NBLK, gpb * N_B, gpb * P_B).astype(BF16)

    return dict(ar=ab_re.reshape(depth, 1, D_STATE), ai=ab_im.reshape(depth, 1, D_STATE),
                bre=pack_b(bb_re), bim=pack_b(bb_im), cre=pack_c(c_re), cim=pack_c(c_im),
                d=d_skip.reshape(depth, 1, D_B))


def _gla_kernel(q_ref, k_ref, v_ref, r_ref, fc_ref, wfg_ref, bfg_ref, gn_ref, s0_ref, y_ref, sfin_ref, st,
                *, chunk, nchunk, bb):
    tb = pl.program_id(1)

    @pl.when(tb == 0)
    def _():
        for j in range(bb):
            for h in range(H_C):
                st[j * H_C + h] = s0_ref[j, h].T

    rows_blk = chunk * nchunk * bb
    la = jax.nn.log_sigmoid(_dot(fc_ref[...].astype(BF16), wfg_ref[...]) + bfg_ref[...]) / GLA_TAU
    ri = lax.broadcasted_iota(jnp.int32, (rows_blk, rows_blk), 0)
    ci = lax.broadcasted_iota(jnp.int32, (rows_blk, rows_blk), 1)
    sh = chunk.bit_length() - 1
    same = (ri >> sh) == (ci >> sh)
    tri = jnp.where(same & (ci <= ri), 1.0, 0.0).astype(BF16)
    la_hi = la.astype(BF16)
    la_lo = (la - la_hi.astype(F32)).astype(BF16)
    bcum = _dot(tri, la_hi) + _dot(tri, la_lo)
    causal = lax.broadcasted_iota(jnp.int32, (chunk, chunk), 1) <= lax.broadcasted_iota(jnp.int32, (chunk, chunk), 0)
    gn = gn_ref[...]
    for c in range(nchunk * bb):
        j = c // nchunk
        rs = slice(c * chunk, (c + 1) * chunk)
        b = bcum[rs]
        bl = b[chunk - 1:chunk]
        qt = (q_ref[rs, :] * jnp.exp(b)).astype(BF16)
        kt = (k_ref[rs, :] * jnp.exp(-b)).astype(BF16)
        kd = (k_ref[rs, :] * jnp.exp(bl - b)).astype(BF16)
        dec = jnp.exp(bl)
        vb = v_ref[rs, :].astype(BF16)
        for h in range(H_C):
            ks = slice(h * DK_C, (h + 1) * DK_C)
            vs = slice(h * DV_C, (h + 1) * DV_C)
            s_t = st[j * H_C + h]
            att = jnp.where(causal, _dot_nt(qt[:, ks], kt[:, ks]), 0.0)
            o = _dot(att.astype(BF16), vb[:, vs]) + _dot_nt(qt[:, ks], s_t.astype(BF16))
            st[j * H_C + h] = s_t * dec[:, ks] + _dot_tn(vb[:, vs], kd[:, ks])
            y_ref[rs, vs] = _rms(o, gn) * jax.nn.silu(r_ref[rs, vs])

    @pl.when(tb == pl.num_programs(1) - 1)
    def _():
        for j in range(bb):
            for h in range(H_C):
                sfin_ref[j, h] = st[j * H_C + h].T


GLA_BLOCK_CHUNKS = 4


def _gla(qc, kc, vc, rc, fc, wfg, bfg, gnorm, layer, s0, batch, seq):
    chunk = math.gcd(seq, GLA_CHUNK)
    nchunk = min(seq // chunk, GLA_BLOCK_CHUNKS)
    nt = seq // (chunk * nchunk)
    bb = math.gcd(batch, max(1, GLA_CHUNK // (chunk * nchunk))) if nt == 1 else 1
    rows_blk = chunk * nchunk * bb
    row = lambda b, t: (b * nt + t, 0)
    st_spec = pl.BlockSpec((bb, H_C, DK_C, DV_C), lambda b, t: (b, 0, 0, 0))
    return pl.pallas_call(
        functools.partial(_gla_kernel, chunk=chunk, nchunk=nchunk, bb=bb),
        grid=(batch // bb, nt),
        in_specs=[pl.BlockSpec((rows_blk, H_C * DK_C), row), pl.BlockSpec((rows_blk, H_C * DK_C), row),
                  pl.BlockSpec((rows_blk, H_C * DV_C), row), pl.BlockSpec((rows_blk, H_C * DV_C), row),
                  pl.BlockSpec((rows_blk, FG_PAD), row),
                  _resident((FG_PAD, H_C * DK_C), layer), _resident((1, H_C * DK_C), layer),
                  _resident((1, DV_C), layer), st_spec],
        out_specs=[pl.BlockSpec((rows_blk, H_C * DV_C), row), st_spec],
        out_shape=[jax.ShapeDtypeStruct((batch * seq, H_C * DV_C), F32),
                   jax.ShapeDtypeStruct((batch, H_C, DK_C, DV_C), F32)],
        scratch_shapes=[pltpu.VMEM((bb * H_C, DV_C, DK_C), F32)],
        compiler_params=_cparams(("parallel", "arbitrary")),
        name="gla",
    )(qc, kc, vc, rc, fc, wfg, bfg, gnorm, s0)


def _post_kernel(x_ref, oa_ref, gates_ref, yb_ref, yc_ref, woa_ref, woc_ref, wout_ref, gmlp_ref, wup_ref, wdn_ref,
                 gfin_ref, out_ref, *, final):
    y_a = _dot(oa_ref[...].astype(BF16), woa_ref[...])
    y_c = _dot(yc_ref[...].astype(BF16), woc_ref[...])
    gates = gates_ref[...]
    m = (jax.nn.sigmoid(gates[:, :D_MODEL]) * y_a + jax.nn.sigmoid(gates[:, D_MODEL:2 * D_MODEL]) * yb_ref[...]
         + jax.nn.sigmoid(gates[:, 2 * D_MODEL:]) * y_c)
    x1 = x_ref[...] + _dot(m.astype(BF16), wout_ref[...])
    hm = _rms(x1, gmlp_ref[...]).astype(BF16)
    up = _dot(hm, wup_ref[...])
    act = jnp.square(jnp.maximum(up, 0.0)).astype(BF16)
    x2 = x1 + _dot(act, wdn_ref[...])
    out_ref[...] = _rms(x2, gfin_ref[...]) if final else x2


def _post(x, oa, gates, yb, yc, lw, layer, gfin, final, tm):
    m = x.shape[0]
    row = lambda i: (i, 0)
    tok = lambda wd: pl.BlockSpec((tm, wd), row)
    return pl.pallas_call(
        functools.partial(_post_kernel, final=final),
        grid=(m // tm,),
        in_specs=[tok(D_MODEL), tok(D_A), tok(3 * D_MODEL), tok(D_MODEL), tok(H_C * DV_C),
                  _resident((D_A, D_MODEL), layer), _resident((H_C * DV_C, D_MODEL), layer),
                  _resident((D_MODEL, D_MODEL), layer), _resident((1, D_MODEL), layer),
                  _resident((D_MODEL, D_FF), layer), _resident((D_FF, D_MODEL), layer),
                  _resident((1, D_MODEL))],
        out_specs=tok(D_MODEL),
        out_shape=jax.ShapeDtypeStruct((m, D_MODEL), F32),
        compiler_params=_cparams(("parallel",)),
        name="post",
    )(x, oa, gates, yb, yc, lw["woa"], lw["woc"], lw["wout"], lw["gmlp"], lw["wup"], lw["wdn"], gfin)


def _t5_bucket(dist):
    exact = N_BUCKETS // 2
    d = np.maximum(dist, 1).astype(np.float32)
    large = exact + (np.log(d / exact) / np.log(W_MAX / exact) * (N_BUCKETS - exact)).astype(np.int32)
    large = np.minimum(large, N_BUCKETS - 1)
    return np.where(dist < exact, dist, large).astype(np.int32)


def _toeplitz(u, rows, cols, off):
    period = cols + off + 1
    w = jnp.pad(u, ((0, 0), (0, period - u.shape[1])))
    a = jnp.tile(w, (1, rows))[:, :rows * (period - 1)].reshape(u.shape[0], rows, period - 1)
    return a[:, :, off:off + cols]


def _prompt_bias(rel_bias, g, dil):
    x = np.arange(3 * N_BACK - 1)
    bk = _t5_bucket(np.clip(2 * N_BACK - 1 - x, 0, N_BACK) * dil)
    u = rel_bias[bk][:, g * H_A:(g + 1) * H_A].T
    return _toeplitz(u, N_BACK, 2 * N_BACK, N_BACK - 1)


def _sample_bias(rel_bias, wbuf, t_new):
    n = wbuf + t_new
    x = np.arange(n + t_new - 1)
    bk = _t5_bucket(np.clip(n - 1 - x, 0, W_MAX))
    tab = rel_bias[bk]
    per_g = [_toeplitz(tab[:, g * H_A:(g + 1) * H_A].T, t_new, n, t_new - 1) for g in range(G_A)]
    full = jnp.concatenate(per_g, axis=1)
    return full[:, :, :wbuf], full[:, :, wbuf:]


def _weights(w_in, w_o_a, w_glu, w_fg2, b_fg, gla_norm, w_o_c, w_out, norm_mix, norm_mlp, w_up, w_down):
    depth = w_in.shape[0]
    fg0 = SEG_OFFS[8]
    win = jnp.concatenate([w_in[:, :, :fg0 + FG_RANK], jnp.zeros((depth, D_MODEL, FG_PAD - FG_RANK), F32),
                           w_in[:, :, fg0 + FG_RANK:]], axis=2).astype(BF16)
    return dict(
        win=win,
        gmix=norm_mix.reshape(depth, 1, D_MODEL),
        wglu=w_glu.astype(BF16),
        wfg=jnp.pad(w_fg2, ((0, 0), (0, FG_PAD - FG_RANK), (0, 0))).astype(BF16),
        bfg=b_fg.reshape(depth, 1, H_C * DK_C),
        gnorm=gla_norm.reshape(depth, 1, DV_C),
        woa=w_o_a.astype(BF16), woc=w_o_c.astype(BF16), wout=w_out.astype(BF16),
        gmlp=norm_mlp.reshape(depth, 1, D_MODEL),
        wup=w_up.astype(BF16), wdn=w_down.astype(BF16),
    )


def kernel(x_prompt, x_sample, cache_k_win, cache_v_win, state_ssm_re, state_ssm_im, state_gla, rel_bias, norm_mix, w_in, w_o_a, s5_log_dt, s5_a_re, s5_a_im, s5_b_re, s5_b_im, s5_c_re, s5_c_im, s5_d, w_glu, w_fg2, b_fg, gla_norm, w_o_c, w_out, norm_mlp, w_up, w_down, norm_final):
    bp, sp, _ = x_prompt.shape
    bs, ts, _ = x_sample.shape
    wbuf = cache_k_win.shape[2]
    mp, ms = bp * sp, bs * ts
    gfin = norm_final.reshape(1, D_MODEL)
    kt_cache = jnp.transpose(cache_k_win, (0, 1, 3, 4, 2))
    vt_cache = jnp.transpose(cache_v_win, (0, 1, 3, 4, 2))
    pbias = [_prompt_bias(rel_bias, g, dil) for g, (_, dil) in enumerate(PATTERNS)]
    sbias_c, sbias_n = _sample_bias(rel_bias, wbuf, ts)

    xp = x_prompt.reshape(mp, D_MODEL)
    xs = x_sample.reshape(ms, D_MODEL)
    outs = {k: [] for k in ("kp", "vp", "kn", "vn", "rp", "ip", "rs", "is", "gp", "gs")}
    lw = _weights(w_in, w_o_a, w_glu, w_fg2, b_fg, gla_norm, w_o_c, w_out, norm_mix, norm_mlp, w_up, w_down)
    s5p = _s5_params(s5_log_dt, s5_a_re, s5_a_im, s5_b_re, s5_b_im, s5_c_re, s5_c_im, s5_d)
    assert sp <= W_MAX
    for l in range(DEPTH):
        final = l == DEPTH - 1

        qa, ka, va, ub, qc, kc, vc, rc, fc, gates, kt, vt = _in_proj(xp, lw["gmix"], lw["win"], l, TM, (bp, sp))
        oa = _attn_prompt(qa, ka, va, pbias, bp, sp)
        zeros = jnp.zeros((bp, D_STATE), F32)
        yb, xr, xi = _s5(ub.reshape(bp, sp, D_B), zeros, zeros, s5p, lw["wglu"], l, S5_TC)
        yb = yb.reshape(mp, D_MODEL)
        yc, gst = _gla(qc, kc, vc, rc, fc, lw["wfg"], lw["bfg"], lw["gnorm"], l,
                       jnp.zeros((bp, H_C, DK_C, DV_C), F32), bp, sp)
        xp = _post(xp, oa, gates, yb, yc, lw, l, gfin, final, TM)
        outs["kp"].append(kt)
        outs["vp"].append(vt)
        outs["rp"].append(xr.reshape(bp, G_B, N_B))
        outs["ip"].append(xi.reshape(bp, G_B, N_B))
        outs["gp"].append(gst)

        qa, ka, va, ub, qc, kc, vc, rc, fc, gates = _in_proj(xs, lw["gmix"], lw["win"], l, TM)
        q4 =qa.reshape(bs, ts, G_A, H_A, HD_A).transpose(0, 3, 2, 1, 4).reshape(bs, H_A, G_A * ts, HD_A)
        knt = ka.reshape(bs, ts, H_A, HD_A).transpose(0, 2, 3, 1)
        vnt = va.reshape(bs, ts, H_A, HD_A).transpose(0, 2, 3, 1)
        o4 = _attn_sample(q4, kt_cache, vt_cache, l, knt, vnt, sbias_c, sbias_n)
        oa = o4.transpose(0, 2, 1, 3).reshape(ms, D_A)
        yb, xr, xi = _s5(ub.reshape(bs, ts, D_B), state_ssm_re[l].reshape(bs, D_STATE),
                         state_ssm_im[l].reshape(bs, D_STATE), s5p, lw["wglu"], l, ts)
        yb = yb.reshape(ms, D_MODEL)
        yc, gst = _gla(qc, kc, vc, rc, fc, lw["wfg"], lw["bfg"], lw["gnorm"], l, state_gla[l], bs, ts)
        xs = _post(xs, oa, gates, yb, yc, lw, l, gfin, final, TM)
        outs["kn"].append(knt)
        outs["vn"].append(vnt)
        outs["rs"].append(xr.reshape(bs, G_B, N_B))
        outs["is"].append(xi.reshape(bs, G_B, N_B))
        outs["gs"].append(gst)

    st = {k: jnp.stack(v) for k, v in outs.items()}
    tail = ((0, 0), (0, 0), (0, 0), (0, 0), (LANES - ts, 0))
    kt_new, vt_new = _cache_update(kt_cache, vt_cache, jnp.pad(st["kn"], tail), jnp.pad(st["vn"], tail), ts)
    to_rows = lambda t: jnp.transpose(t, (0, 1, 4, 2, 3))
    return (xp.reshape(bp, sp, D_MODEL), xs.reshape(bs, ts, D_MODEL),
            to_rows(st["kp"]), to_rows(st["vp"]), to_rows(kt_new), to_rows(vt_new),
            st["rp"], st["ip"], st["rs"], st["is"], st["gp"], st["gs"])
```

```python
import functools
import math

import numpy as np
import jax
import jax.numpy as jnp
from jax import lax
from jax.experimental import pallas as pl
from jax.experimental.pallas import tpu as pltpu

F32 = jnp.float32
BF16 = jnp.bfloat16

LANES = 128
D_MODEL = 1024
DEPTH = 4
H_A = 8
HD_A = 64
D_A = H_A * HD_A
N_PAIR = D_A // LANES
PATTERNS = ((128, 1), (512, 4), (2048, 16))
G_A = len(PATTERNS)
N_BACK = 128
W_MAX = 2048
N_BUCKETS = 32
P_B = 16
D_B = 512
G_B = D_B // P_B
N_B = 64
D_STATE = G_B * N_B
H_C = 4
DK_C = 64
DV_C = 128
FG_RANK = 16
FG_PAD = 128
GLA_TAU = 16.0
GLA_CHUNK = 64
D_FF = 4 * D_MODEL
EPS = 1e-6
NEG = -1e30
LOG2E = 1.4426950408889634

SEG_WIDTHS = (G_A * D_A, D_A, D_A, D_B, H_C * DK_C, H_C * DK_C, H_C * DV_C, H_C * DV_C, FG_RANK, 3 * D_MODEL)
SEG_OFFS = tuple(int(v) for v in np.cumsum((0,) + SEG_WIDTHS))
IN_COLS = SEG_OFFS[-1]

VMEM_LIMIT = 56 * 1024 * 1024
TM = 256


def _cparams(sem):
    return pltpu.CompilerParams(dimension_semantics=sem, vmem_limit_bytes=VMEM_LIMIT)


def _resident(shape, layer=None):
    nd = len(shape)
    if layer is None:
        return pl.BlockSpec(shape, lambda *_: (0,) * nd, pipeline_mode=pl.Buffered(1))
    return pl.BlockSpec((None,) + tuple(shape), lambda *_: (layer,) + (0,) * nd, pipeline_mode=pl.Buffered(1))


def _rms(x, g):
    return x * lax.rsqrt(jnp.mean(x * x, axis=-1, keepdims=True) + EPS) * g


def _dot(a, b):
    return jnp.dot(a, b, preferred_element_type=F32)


def _dot_nt(a, b):
    return lax.dot_general(a, b, (((1,), (1,)), ((), ())), preferred_element_type=F32)


def _dot_tn(a, b):
    return lax.dot_general(a, b, (((0,), (0,)), ((), ())), preferred_element_type=F32)


def _in_proj_kernel(x_ref, g_ref, w_ref, *out_refs):
    hb = _rms(x_ref[...], g_ref[...]).astype(BF16)
    for s, o_ref in enumerate(out_refs[:len(SEG_WIDTHS)]):
        z = _dot_nt(hb, w_ref[SEG_OFFS[s]:SEG_OFFS[s + 1], :])
        if s in (0, 4):
            z = z * 0.125
        o_ref[...] = z
        if s in (1, 2) and len(out_refs) > len(SEG_WIDTHS):
            out_refs[len(SEG_WIDTHS) + s - 1][0] = z.T.reshape(H_A, HD_A, z.shape[0])


def _in_proj(x, g, w, layer, tm, kv_t_shape=None):
    m = x.shape[0]
    row = lambda i: (i, 0)
    out_specs = [pl.BlockSpec((tm, wd), row) for wd in SEG_WIDTHS]
    out_shape = [jax.ShapeDtypeStruct((m, wd), F32) for wd in SEG_WIDTHS]
    if kv_t_shape is not None:
        batch, seq = kv_t_shape
        per = seq // tm
        out_specs += [pl.BlockSpec((1, H_A, HD_A, tm), lambda i: (i // per, 0, 0, i % per))] * 2
        out_shape += [jax.ShapeDtypeStruct((batch, H_A, HD_A, seq), F32)] * 2
    return pl.pallas_call(
        _in_proj_kernel,
        grid=(m // tm,),
        in_specs=[pl.BlockSpec((tm, D_MODEL), row), _resident((1, D_MODEL), layer),
                  _resident((IN_COLS, D_MODEL), layer)],
        out_specs=out_specs,
        out_shape=out_shape,
        compiler_params=_cparams(("parallel",)),
        name="in_proj",
    )(x, g, w)


ATTN_UNROLL = 8


def _attn_prompt_kernel(q0_ref, q1_ref, q2_ref, k_ref, v_ref, b0_ref, b1_ref, b2_ref, o_ref,
                        qs, k1, v1, k4, v4, k16, v16, tq, tk, tv, og, lg, uo, ul, *, seq):
    nu = seq // N_BACK
    c4 = seq // 4
    left = lax.broadcasted_iota(jnp.int32, (N_BACK, LANES), 1) < HD_A

    def put_q(g, u, q):
        q = q * LOG2E
        qs[g, 2 * N_BACK * u:2 * N_BACK * u + N_BACK, :] = jnp.where(left, q, 0.0).astype(BF16)
        qs[g, 2 * N_BACK * u + N_BACK:2 * N_BACK * (u + 1), :] = jnp.where(left, 0.0, q).astype(BF16)

    @pl.when((pl.program_id(0) == 0) & (pl.program_id(1) == 0))
    def _():
        zpad = jnp.zeros((N_BACK, LANES), BF16)
        for buf in (k1, k4, v1, v4):
            buf[0:N_BACK, 0:LANES] = zpad
        for buf in (v1, v4, v16):
            buf[:, LANES:] = jnp.ones((buf.shape[0], LANES), BF16)

    k1[N_BACK:N_BACK + seq, :] = k_ref[...].astype(BF16)
    v1[N_BACK:N_BACK + seq, 0:LANES] = v_ref[...].astype(BF16)
    for r in range(4):
        by4 = pl.ds(r, c4, stride=4)
        tk[r * c4:(r + 1) * c4, :] = k_ref[by4, :]
        tv[r * c4:(r + 1) * c4, :] = v_ref[by4, :]
        tq[r * c4:(r + 1) * c4, :] = q2_ref[by4, :]
    k4[N_BACK:N_BACK + seq, :] = tk[...].astype(BF16)
    v4[N_BACK:N_BACK + seq, 0:LANES] = tv[...].astype(BF16)
    for u in range(nu):
        put_q(0, u, q0_ref[u * N_BACK:(u + 1) * N_BACK, :])
        put_q(1, u, q1_ref[pl.ds(u // 4 + 4 * N_BACK * (u % 4), N_BACK, stride=4), :])
        by16 = pl.ds((u % 4) * c4 + u // 4, N_BACK, stride=4)
        put_q(2, u, tq[by16, :])
        k16[u * N_BACK:(u + 1) * N_BACK, :] = tk[by16, :].astype(BF16)
        v16[u * N_BACK:(u + 1) * N_BACK, 0:LANES] = tv[by16, :].astype(BF16)

    shp = (2 * N_BACK, 2 * N_BACK)
    qi = lax.broadcasted_iota(jnp.int32, shp, 0) & (N_BACK - 1)
    kk = lax.broadcasted_iota(jnp.int32, shp, 1)
    step = N_BACK + qi - kk
    band = (step >= 0) & (step <= N_BACK)
    tri = band[:, N_BACK:]

    def unit(u, carry, *, g, kbuf, vbuf, b_ref, class_blocks):
        q = qs[g, pl.ds(pl.multiple_of(u * 2 * N_BACK, 2 * N_BACK), 2 * N_BACK), :]
        k0 = pl.multiple_of(u * N_BACK, N_BACK)
        if class_blocks > 1:
            krows = pl.ds(k0, 2 * N_BACK)
            valid = band
            first = (u & (class_blocks - 1)) == 0
            bias = b_ref[0, pl.ds(pl.multiple_of(jnp.where(first, 2 * N_BACK, 0), 2 * N_BACK), 2 * N_BACK), :]
        else:
            krows = pl.ds(k0, N_BACK)
            valid = tri
            bias = b_ref[0, 0:2 * N_BACK, N_BACK:]
        s = jnp.where(valid, _dot_nt(q, kbuf[krows, :]) + bias, NEG)
        mx = jnp.max(s, axis=-1, keepdims=True)
        p = jnp.exp2(s - mx)
        pv = _dot(p.astype(BF16), vbuf[krows, :])
        den = pv[:, LANES:]
        o = pv[:, 0:LANES] / den
        lse = mx + jnp.log2(den)
        og[g, pl.ds(k0, N_BACK), :] = jnp.where(left, o[0:N_BACK], o[N_BACK:])
        lg[g, pl.ds(k0, N_BACK), :] = jnp.where(left, lse[0:N_BACK], lse[N_BACK:])
        return carry

    for g, (kbuf, vbuf, b_ref, class_blocks) in enumerate(((k1, v1, b0_ref, nu), (k4, v4, b1_ref, nu // 4),
                                                           (k16, v16, b2_ref, 1))):
        lax.fori_loop(0, nu, functools.partial(unit, g=g, kbuf=kbuf, vbuf=vbuf, b_ref=b_ref,
                                               class_blocks=class_blocks), 0, unroll=ATTN_UNROLL)

    for r in range(4):
        by4 = pl.ds(r, c4, stride=4)
        uo[0, by4, :] = og[1, r * c4:(r + 1) * c4, :]
        ul[0, by4, :] = lg[1, r * c4:(r + 1) * c4, :]
    for u in range(nu):
        by16 = pl.ds((u % 4) * c4 + u // 4, N_BACK, stride=4)
        tq[by16, :] = og[2, u * N_BACK:(u + 1) * N_BACK, :]
        tk[by16, :] = lg[2, u * N_BACK:(u + 1) * N_BACK, :]
    for r in range(4):
        by4 = pl.ds(r, c4, stride=4)
        uo[1, by4, :] = tq[r * c4:(r + 1) * c4, :]
        ul[1, by4, :] = tk[r * c4:(r + 1) * c4, :]

    def merge(i, carry):
        rows = pl.ds(pl.multiple_of(i * N_BACK, N_BACK), N_BACK)
        ls = [lg[0, rows, :], ul[0, rows, :], ul[1, rows, :]]
        mx = jnp.maximum(jnp.maximum(ls[0], ls[1]), ls[2])
        ws = [jnp.exp2(l - mx) for l in ls]
        num = ws[0] * og[0, rows, :] + ws[1] * uo[0, rows, :] + ws[2] * uo[1, rows, :]
        o_ref[rows, :] = num / (ws[0] + ws[1] + ws[2])
        return carry

    lax.fori_loop(0, nu, merge, 0)


def _attn_prompt(qa, ka, va, biasmats, batch, seq):
    assert tuple(d for _, d in PATTERNS) == (1, 4, 16) and seq == 16 * N_BACK
    blk = (seq, LANES)
    qspec = lambda g: pl.BlockSpec(blk, lambda b, p: (b, g * N_PAIR + p))
    kspec = pl.BlockSpec(blk, lambda b, p: (b, p))
    bspec = pl.BlockSpec((1, 4 * N_BACK, 2 * N_BACK), lambda b, p: (p, 0, 0))
    before = np.arange(2 * N_BACK) < N_BACK
    pair_bias = []
    for bm in biasmats:
        t = (bm * LOG2E).reshape(N_PAIR, 2 * N_BACK, 2 * N_BACK)
        pair_bias.append(jnp.concatenate([t, jnp.where(before, NEG, t)], axis=1))
    act = lambda rows: pltpu.VMEM((rows, LANES), BF16)
    val = lambda rows: pltpu.VMEM((rows, 2 * LANES), BF16)
    return pl.pallas_call(
        functools.partial(_attn_prompt_kernel, seq=seq),
        grid=(batch, N_PAIR),
        in_specs=[qspec(0), qspec(1), qspec(2), kspec, kspec, bspec, bspec, bspec],
        out_specs=kspec,
        out_shape=jax.ShapeDtypeStruct((batch * seq, D_A), F32),
        scratch_shapes=[pltpu.VMEM((G_A, 2 * seq, LANES), BF16),
                        act(N_BACK + seq), val(N_BACK + seq), act(N_BACK + seq), val(N_BACK + seq),
                        act(seq), val(seq),
                        pltpu.VMEM((seq, LANES), F32), pltpu.VMEM((seq, LANES), F32), pltpu.VMEM((seq, LANES), F32),
                        pltpu.VMEM((G_A, seq, LANES), F32), pltpu.VMEM((G_A, seq, LANES), F32),
                        pltpu.VMEM((2, seq, LANES), F32), pltpu.VMEM((2, seq, LANES), F32)],
        compiler_params=_cparams(("arbitrary", "arbitrary")),
        name="attn_prompt",
    )(qa, qa, qa, ka, va, *pair_bias)


def _attn_sample_kernel(q_ref, kt_ref, vt_ref, knt_ref, vnt_ref, bc_ref, bn_ref, o_ref, *, wbuf, t_new):
    rows = G_A * t_new

    def allowed(shape, key0):
        rr = lax.broadcasted_iota(jnp.int32, shape, 0)
        col = lax.broadcasted_iota(jnp.int32, shape, 1)
        grp = rr >> (t_new.bit_length() - 1)
        dist = wbuf + (rr - grp * t_new) - (col + key0)
        dm1 = jnp.where(grp == 0, PATTERNS[0][1] - 1, jnp.where(grp == 1, PATTERNS[1][1] - 1, PATTERNS[2][1] - 1))
        win = jnp.where(grp == 0, PATTERNS[0][0], jnp.where(grp == 1, PATTERNS[1][0], PATTERNS[2][0]))
        return (dist >= 0) & ((dist & dm1) == 0) & (dist <= win)

    ok_c = allowed((rows, wbuf), 0)
    ok_n = allowed((rows, t_new), wbuf)
    for h in range(H_A):
        qh = q_ref[0, h].astype(BF16)
        sc = jnp.where(ok_c, _dot(qh, kt_ref[0, 0, h].astype(BF16)) + bc_ref[h], NEG)
        sn = jnp.where(ok_n, _dot(qh, knt_ref[0, h].astype(BF16)) + bn_ref[h], NEG)
        mx = jnp.maximum(jnp.max(sc, axis=-1, keepdims=True), jnp.max(sn, axis=-1, keepdims=True))
        mt = jnp.maximum(jnp.maximum(mx[0:t_new], mx[t_new:2 * t_new]), mx[2 * t_new:3 * t_new])
        mall = jnp.concatenate([mt, mt, mt], axis=0)
        pc = jnp.exp(sc - mall)
        pn = jnp.exp(sn - mall)
        den = jnp.sum(pc, axis=-1, keepdims=True) + jnp.sum(pn, axis=-1, keepdims=True)
        acc = (_dot_nt(pc.astype(BF16), vt_ref[0, 0, h].astype(BF16))
               + _dot_nt(pn.astype(BF16), vnt_ref[0, h].astype(BF16)))
        dt = den[0:t_new] + den[t_new:2 * t_new] + den[2 * t_new:3 * t_new]
        at = acc[0:t_new] + acc[t_new:2 * t_new] + acc[2 * t_new:3 * t_new]
        o_ref[0, h] = at / dt


def _attn_sample(q, kt_cache, vt_cache, layer, knt, vnt, bias_c, bias_n):
    batch, wbuf = kt_cache.shape[1], kt_cache.shape[4]
    t_new = knt.shape[3]
    rows = G_A * t_new
    b4 = lambda b: (b, 0, 0, 0)
    cache_spec = pl.BlockSpec((1, 1, H_A, HD_A, wbuf), lambda b: (layer, b, 0, 0, 0))
    new_spec = pl.BlockSpec((1, H_A, HD_A, t_new), b4)
    return pl.pallas_call(
        functools.partial(_attn_sample_kernel, wbuf=wbuf, t_new=t_new),
        grid=(batch,),
        in_specs=[pl.BlockSpec((1, H_A, rows, HD_A), b4), cache_spec, cache_spec, new_spec, new_spec,
                  _resident((H_A, rows, wbuf)), _resident((H_A, rows, t_new))],
        out_specs=pl.BlockSpec((1, H_A, t_new, HD_A), b4),
        out_shape=jax.ShapeDtypeStruct((batch, H_A, t_new, HD_A), F32),
        compiler_params=_cparams(("parallel",)),
        name="attn_sample",
    )(q, kt_cache, vt_cache, knt, vnt, bias_c, bias_n)


def _cache_update_kernel(kt_ref, vt_ref, kn_ref, vn_ref, ko_ref, vo_ref, *, wbuf, t_new):
    is_new = lax.broadcasted_iota(jnp.int32, (HD_A, LANES), 1) >= LANES - t_new
    for src, new, dst in ((kt_ref, kn_ref, ko_ref), (vt_ref, vn_ref, vo_ref)):
        for h in range(H_A):
            moved = pltpu.roll(src[0, 0, h], wbuf - t_new, 1)
            dst[0, 0, h, :, 0:wbuf - LANES] = moved[:, 0:wbuf - LANES]
            dst[0, 0, h, :, wbuf - LANES:wbuf] = jnp.where(is_new, new[0, 0, h], moved[:, wbuf - LANES:wbuf])


def _cache_update(kt_cache, vt_cache, kn_tail, vn_tail, t_new):
    depth, batch, _, _, wbuf = kt_cache.shape
    idx = lambda l, b: (l, b, 0, 0, 0)
    cache_spec = pl.BlockSpec((1, 1, H_A, HD_A, wbuf), idx)
    tail_spec = pl.BlockSpec((1, 1, H_A, HD_A, LANES), idx)
    shape = jax.ShapeDtypeStruct(kt_cache.shape, F32)
    return pl.pallas_call(
        functools.partial(_cache_update_kernel, wbuf=wbuf, t_new=t_new),
        grid=(depth, batch),
        in_specs=[cache_spec, cache_spec, tail_spec, tail_spec],
        out_specs=[cache_spec, cache_spec],
        out_shape=[shape, shape],
        compiler_params=_cparams(("parallel", "parallel")),
        name="cache_update",
    )(kt_cache, vt_cache, kn_tail, vn_tail)


S5_KB = 128
S5_NBLK = D_B // S5_KB
S5_SB = D_STATE // S5_NBLK
S5_TC = 64
S5_SCAN_UNROLL = 8


def _s5_kernel(u_ref, x0r_ref, x0i_ref, ar_ref, ai_ref, bre_ref, bim_ref, cre_ref, cim_ref, d_ref, wglu_ref,
               yb_ref, xr_out, xi_out, ut, ybt, bur, bui, sr, si, *, nb, tc):
    @pl.when(pl.program_id(0) == 0)
    def _():
        sr[...] = x0r_ref[...]
        si[...] = x0i_ref[...]

    for b in range(nb):
        for c in range(D_B // LANES):
            ut[c, pl.ds(b, tc, stride=nb), :] = u_ref[b, :, c * LANES:(c + 1) * LANES]
    u = jnp.concatenate([ut[c] for c in range(D_B // LANES)], axis=1)
    ub = u.astype(BF16)
    for j in range(S5_NBLK):
        uj = ub[:, j * S5_KB:(j + 1) * S5_KB]
        bur[:, j * S5_SB:(j + 1) * S5_SB] = _dot(uj, bre_ref[j])
        bui[:, j * S5_SB:(j + 1) * S5_SB] = _dot(uj, bim_ref[j])

    ar = ar_ref[...]
    ai = ai_ref[...]

    def step(t, carry):
        xr, xi = carry
        rows = pl.ds(pl.multiple_of(t * nb, nb), nb)
        nr = ar * xr - ai * xi + bur[rows, :]
        ni = ar * xi + ai * xr + bui[rows, :]
        bur[rows, :] = nr
        bui[rows, :] = ni
        return nr, ni

    xr, xi = lax.fori_loop(0, tc, step, (sr[...], si[...]), unroll=S5_SCAN_UNROLL)
    sr[...] = xr
    si[...] = xi
    xr_out[...] = xr
    xi_out[...] = xi

    ys = []
    for j in range(S5_NBLK):
        cs = slice(j * S5_SB, (j + 1) * S5_SB)
        ys.append(_dot(bur[:, cs].astype(BF16), cre_ref[j]) - _dot(bui[:, cs].astype(BF16), cim_ref[j]))
    y = jnp.concatenate(ys, axis=-1) + d_ref[...] * u
    ab = _dot(jax.nn.gelu(y).astype(BF16), wglu_ref[...])
    for c in range(D_MODEL // LANES):
        cs = slice(c * LANES, (c + 1) * LANES)
        ybt[c] = ab[:, cs] * jax.nn.sigmoid(ab[:, D_MODEL + c * LANES:D_MODEL + (c + 1) * LANES])
    for b in range(nb):
        for c in range(D_MODEL // LANES):
            yb_ref[b, :, c * LANES:(c + 1) * LANES] = ybt[c, pl.ds(b, tc, stride=nb), :]


def _s5(u, x0r, x0i, prm, wglu, layer, tc):
    nb, seq, _ = u.shape
    r = tc * nb
    blk = lambda i: (0, i, 0)
    st = pl.BlockSpec((nb, D_STATE), lambda i: (0, 0))
    return pl.pallas_call(
        functools.partial(_s5_kernel, nb=nb, tc=tc),
        grid=(seq // tc,),
        in_specs=[pl.BlockSpec((nb, tc, D_B), blk), st, st,
                  _resident((1, D_STATE), layer), _resident((1, D_STATE), layer),
                  _resident((S5_NBLK, S5_KB, S5_SB), layer), _resident((S5_NBLK, S5_KB, S5_SB), layer),
                  _resident((S5_NBLK, S5_SB, S5_KB), layer), _resident((S5_NBLK, S5_SB, S5_KB), layer),
                  _resident((1, D_B), layer), _resident((D_B, 2 * D_MODEL), layer)],
        out_specs=[pl.BlockSpec((nb, tc, D_MODEL), blk), st, st],
        out_shape=[jax.ShapeDtypeStruct((nb, seq, D_MODEL), F32),
                   jax.ShapeDtypeStruct((nb, D_STATE), F32), jax.ShapeDtypeStruct((nb, D_STATE), F32)],
        scratch_shapes=[pltpu.VMEM((D_B // LANES, r, LANES), F32), pltpu.VMEM((D_MODEL // LANES, r, LANES), F32),
                        pltpu.VMEM((r, D_STATE), F32), pltpu.VMEM((r, D_STATE), F32),
                        pltpu.VMEM((nb, D_STATE), F32), pltpu.VMEM((nb, D_STATE), F32)],
        compiler_params=_cparams(("arbitrary",)),
        name="s5",
    )(u, x0r, x0i, prm["ar"], prm["ai"], prm["bre"], prm["bim"], prm["cre"], prm["cim"], prm["d"], wglu)


def _s5_params(log_dt, a_re, a_im, b_re, b_im, c_re, c_im, d_skip):
    depth = log_dt.shape[0]
    dt = jnp.exp(log_dt)[..., None]
    mag = jnp.exp(a_re * dt)
    ab_re = mag * jnp.cos(a_im * dt)
    ab_im = mag * jnp.sin(a_im * dt)
    den = a_re * a_re + a_im * a_im
    nr = ab_re - 1.0
    co_re = (nr * a_re + ab_im * a_im) / den
    co_im = (ab_im * a_re - nr * a_im) / den
    bb_re = co_re[..., None] * b_re - co_im[..., None] * b_im
    bb_im = co_re[..., None] * b_im + co_im[..., None] * b_re
    gpb = G_B // S5_NBLK
    eye = jnp.eye(gpb, dtype=F32)

    def pack_b(bb):
        t = bb.reshape(depth, S5_NBLK, gpb, N_B, P_B)
        m = jnp.einsum('ljgnp,gh->ljgphn', t, eye)
        return m.reshape(depth, S5_NBLK, gpb * P_B, gpb * N_B).astype(BF16)

    def pack_c(cc):
        t = cc.reshape(depth, S5_NBLK, gpb, P_B, N_B)
        m = jnp.einsum('ljgpn,gh->ljgnhp', t, eye)
        return m.reshape(depth, S5_NBLK, gpb * N_B, gpb * P_B).astype(BF16)

    return dict(ar=ab_re.reshape(depth, 1, D_STATE), ai=ab_im.reshape(depth, 1, D_STATE),
                bre=pack_b(bb_re), bim=pack_b(bb_im), cre=pack_c(c_re), cim=pack_c(c_im),
                d=d_skip.reshape(depth, 1, D_B))


def _gla_kernel(q_ref, k_ref, v_ref, r_ref, fc_ref, wfg_ref, bfg_ref, gn_ref, s0_ref, y_ref, sfin_ref, st,
                *, chunk, nchunk, bb):
    tb = pl.program_id(1)

    @pl.when(tb == 0)
    def _():
        for j in range(bb):
            for h in range(H_C):
                st[j * H_C + h] = s0_ref[j, h].T

    rows_blk = chunk * nchunk * bb
    la = jax.nn.log_sigmoid(_dot(fc_ref[...].astype(BF16), wfg_ref[...]) + bfg_ref[...]) / GLA_TAU
    ri = lax.broadcasted_iota(jnp.int32, (rows_blk, rows_blk), 0)
    ci = lax.broadcasted_iota(jnp.int32, (rows_blk, rows_blk), 1)
    sh = chunk.bit_length() - 1
    same = (ri >> sh) == (ci >> sh)
    tri = jnp.where(same & (ci <= ri), 1.0, 0.0).astype(BF16)
    la_hi = la.astype(BF16)
    la_lo = (la - la_hi.astype(F32)).astype(BF16)
    bcum = _dot(tri, la_hi) + _dot(tri, la_lo)
    causal = lax.broadcasted_iota(jnp.int32, (chunk, chunk), 1) <= lax.broadcasted_iota(jnp.int32, (chunk, chunk), 0)
    gn = gn_ref[...]
    for c in range(nchunk * bb):
        j = c // nchunk
        rs = slice(c * chunk, (c + 1) * chunk)
        b = bcum[rs]
        bl = b[chunk - 1:chunk]
        qt = (q_ref[rs, :] * jnp.exp(b)).astype(BF16)
        kt = (k_ref[rs, :] * jnp.exp(-b)).astype(BF16)
        kd = (k_ref[rs, :] * jnp.exp(bl - b)).astype(BF16)
        dec = jnp.exp(bl)
        vb = v_ref[rs, :].astype(BF16)
        for h in range(H_C):
            ks = slice(h * DK_C, (h + 1) * DK_C)
            vs = slice(h * DV_C, (h + 1) * DV_C)
            s_t = st[j * H_C + h]
            att = jnp.where(causal, _dot_nt(qt[:, ks], kt[:, ks]), 0.0)
            o = _dot(att.astype(BF16), vb[:, vs]) + _dot_nt(qt[:, ks], s_t.astype(BF16))
            st[j * H_C + h] = s_t * dec[:, ks] + _dot_tn(vb[:, vs], kd[:, ks])
            y_ref[rs, vs] = _rms(o, gn) * jax.nn.silu(r_ref[rs, vs])

    @pl.when(tb == pl.num_programs(1) - 1)
    def _():
        for j in range(bb):
            for h in range(H_C):
                sfin_ref[j, h] = st[j * H_C + h].T


GLA_BLOCK_CHUNKS = 4


def _gla(qc, kc, vc, rc, fc, wfg, bfg, gnorm, layer, s0, batch, seq):
    chunk = math.gcd(seq, GLA_CHUNK)
    nchunk = min(seq // chunk, GLA_BLOCK_CHUNKS)
    nt = seq // (chunk * nchunk)
    bb = math.gcd(batch, max(1, GLA_CHUNK // (chunk * nchunk))) if nt == 1 else 1
    rows_blk = chunk * nchunk * bb
    row = lambda b, t: (b * nt + t, 0)
    st_spec = pl.BlockSpec((bb, H_C, DK_C, DV_C), lambda b, t: (b, 0, 0, 0))
    return pl.pallas_call(
        functools.partial(_gla_kernel, chunk=chunk, nchunk=nchunk, bb=bb),
        grid=(batch // bb, nt),
        in_specs=[pl.BlockSpec((rows_blk, H_C * DK_C), row), pl.BlockSpec((rows_blk, H_C * DK_C), row),
                  pl.BlockSpec((rows_blk, H_C * DV_C), row), pl.BlockSpec((rows_blk, H_C * DV_C), row),
                  pl.BlockSpec((rows_blk, FG_RANK), row),
                  _resident((FG_RANK, H_C * DK_C), layer), _resident((1, H_C * DK_C), layer),
                  _resident((1, DV_C), layer), st_spec],
        out_specs=[pl.BlockSpec((rows_blk, H_C * DV_C), row), st_spec],
        out_shape=[jax.ShapeDtypeStruct((batch * seq, H_C * DV_C), F32),
                   jax.ShapeDtypeStruct((batch, H_C, DK_C, DV_C), F32)],
        scratch_shapes=[pltpu.VMEM((bb * H_C, DV_C, DK_C), F32)],
        compiler_params=_cparams(("parallel", "arbitrary")),
        name="gla",
    )(qc, kc, vc, rc, fc, wfg, bfg, gnorm, s0)


def _post_kernel(x_ref, oa_ref, gates_ref, yb_ref, yc_ref, woa_ref, woc_ref, wout_ref, gmlp_ref, wup_ref, wdn_ref,
                 gfin_ref, out_ref, *, final):
    y_a = _dot(oa_ref[...].astype(BF16), woa_ref[...])
    y_c = _dot(yc_ref[...].astype(BF16), woc_ref[...])
    gates = gates_ref[...]
    m = (jax.nn.sigmoid(gates[:, :D_MODEL]) * y_a + jax.nn.sigmoid(gates[:, D_MODEL:2 * D_MODEL]) * yb_ref[...]
         + jax.nn.sigmoid(gates[:, 2 * D_MODEL:]) * y_c)
    x1 = x_ref[...] + _dot(m.astype(BF16), wout_ref[...])
    hm = _rms(x1, gmlp_ref[...]).astype(BF16)
    up = _dot(hm, wup_ref[...])
    act = jnp.square(jnp.maximum(up, 0.0)).astype(BF16)
    x2 = x1 + _dot(act, wdn_ref[...])
    out_ref[...] = _rms(x2, gfin_ref[...]) if final else x2


def _post(x, oa, gates, yb, yc, lw, layer, gfin, final, tm):
    m = x.shape[0]
    row = lambda i: (i, 0)
    tok = lambda wd: pl.BlockSpec((tm, wd), row)
    return pl.pallas_call(
        functools.partial(_post_kernel, final=final),
        grid=(m // tm,),
        in_specs=[tok(D_MODEL), tok(D_A), tok(3 * D_MODEL), tok(D_MODEL), tok(H_C * DV_C),
                  _resident((D_A, D_MODEL), layer), _resident((H_C * DV_C, D_MODEL), layer),
                  _resident((D_MODEL, D_MODEL), layer), _resident((1, D_MODEL), layer),
                  _resident((D_MODEL, D_FF), layer), _resident((D_FF, D_MODEL), layer),
                  _resident((1, D_MODEL))],
        out_specs=tok(D_MODEL),
        out_shape=jax.ShapeDtypeStruct((m, D_MODEL), F32),
        compiler_params=_cparams(("parallel",)),
        name="post",
    )(x, oa, gates, yb, yc, lw["woa"], lw["woc"], lw["wout"], lw["gmlp"], lw["wup"], lw["wdn"], gfin)


def _t5_bucket(dist):
    exact = N_BUCKETS // 2
    d = np.maximum(dist, 1).astype(np.float32)
    large = exact + (np.log(d / exact) / np.log(W_MAX / exact) * (N_BUCKETS - exact)).astype(np.int32)
    large = np.minimum(large, N_BUCKETS - 1)
    return np.where(dist < exact, dist, large).astype(np.int32)


def _toeplitz(u, rows, cols, off):
    period = cols + off + 1
    w = jnp.pad(u, ((0, 0), (0, period - u.shape[1])))
    a = jnp.tile(w, (1, rows))[:, :rows * (period - 1)].reshape(u.shape[0], rows, period - 1)
    return a[:, :, off:off + cols]


def _prompt_bias(rel_bias, g, dil):
    x = np.arange(3 * N_BACK - 1)
    bk = _t5_bucket(np.clip(2 * N_BACK - 1 - x, 0, N_BACK) * dil)
    u = rel_bias[bk][:, g * H_A:(g + 1) * H_A].T
    return _toeplitz(u, N_BACK, 2 * N_BACK, N_BACK - 1)


def _sample_bias(rel_bias, wbuf, t_new):
    n = wbuf + t_new
    x = np.arange(n + t_new - 1)
    bk = _t5_bucket(np.clip(n - 1 - x, 0, W_MAX))
    tab = rel_bias[bk]
    per_g = [_toeplitz(tab[:, g * H_A:(g + 1) * H_A].T, t_new, n, t_new - 1) for g in range(G_A)]
    full = jnp.concatenate(per_g, axis=1)
    return full[:, :, :wbuf], full[:, :, wbuf:]


def _weights(w_in, w_o_a, w_glu, w_fg2, b_fg, gla_norm, w_o_c, w_out, norm_mix, norm_mlp, w_up, w_down):
    depth = w_in.shape[0]
    return dict(
        win=jnp.transpose(w_in, (0, 2, 1)).astype(BF16),
        gmix=norm_mix.reshape(depth, 1, D_MODEL),
        wglu=w_glu.astype(BF16),
        wfg=w_fg2.astype(BF16),
        bfg=b_fg.reshape(depth, 1, H_C * DK_C),
        gnorm=gla_norm.reshape(depth, 1, DV_C),
        woa=w_o_a.astype(BF16), woc=w_o_c.astype(BF16), wout=w_out.astype(BF16),
        gmlp=norm_mlp.reshape(depth, 1, D_MODEL),
        wup=w_up.astype(BF16), wdn=w_down.astype(BF16),
    )


def kernel(x_prompt, x_sample, cache_k_win, cache_v_win, state_ssm_re, state_ssm_im, state_gla, rel_bias, norm_mix, w_in, w_o_a, s5_log_dt, s5_a_re, s5_a_im, s5_b_re, s5_b_im, s5_c_re, s5_c_im, s5_d, w_glu, w_fg2, b_fg, gla_norm, w_o_c, w_out, norm_mlp, w_up, w_down, norm_final):
    bp, sp, _ = x_prompt.shape
    bs, ts, _ = x_sample.shape
    wbuf = cache_k_win.shape[2]
    mp, ms = bp * sp, bs * ts
    gfin = norm_final.reshape(1, D_MODEL)
    kt_cache = jnp.transpose(cache_k_win, (0, 1, 3, 4, 2))
    vt_cache = jnp.transpose(cache_v_win, (0, 1, 3, 4, 2))
    pbias = [_prompt_bias(rel_bias, g, dil) for g, (_, dil) in enumerate(PATTERNS)]
    sbias_c, sbias_n = _sample_bias(rel_bias, wbuf, ts)

    xp = x_prompt.reshape(mp, D_MODEL)
    xs = x_sample.reshape(ms, D_MODEL)
    outs = {k: [] for k in ("kp", "vp", "kn", "vn", "rp", "ip", "rs", "is", "gp", "gs")}
    lw = _weights(w_in, w_o_a, w_glu, w_fg2, b_fg, gla_norm, w_o_c, w_out, norm_mix, norm_mlp, w_up, w_down)
    s5p = _s5_params(s5_log_dt, s5_a_re, s5_a_im, s5_b_re, s5_b_im, s5_c_re, s5_c_im, s5_d)
    assert sp <= W_MAX
    for l in range(DEPTH):
        final = l == DEPTH - 1

        qa, ka, va, ub, qc, kc, vc, rc, fc, gates, kt, vt = _in_proj(xp, lw["gmix"], lw["win"], l, TM, (bp, sp))
        oa = _attn_prompt(qa, ka, va, pbias, bp, sp)
        zeros = jnp.zeros((bp, D_STATE), F32)
        yb, xr, xi = _s5(ub.reshape(bp, sp, D_B), zeros, zeros, s5p, lw["wglu"], l, S5_TC)
        yb = yb.reshape(mp, D_MODEL)
        yc, gst = _gla(qc, kc, vc, rc, fc, lw["wfg"], lw["bfg"], lw["gnorm"], l,
                       jnp.zeros((bp, H_C, DK_C, DV_C), F32), bp, sp)
        xp = _post(xp, oa, gates, yb, yc, lw, l, gfin, final, TM)
        outs["kp"].append(kt)
        outs["vp"].append(vt)
        outs["rp"].append(xr.reshape(bp, G_B, N_B))
        outs["ip"].append(xi.reshape(bp, G_B, N_B))
        outs["gp"].append(gst)

        qa, ka, va, ub, qc, kc, vc, rc, fc, gates = _in_proj(xs, lw["gmix"], lw["win"], l, TM)
        q4 =qa.reshape(bs, ts, G_A, H_A, HD_A).transpose(0, 3, 2, 1, 4).reshape(bs, H_A, G_A * ts, HD_A)
        knt = ka.reshape(bs, ts, H_A, HD_A).transpose(0, 2, 3, 1)
        vnt = va.reshape(bs, ts, H_A, HD_A).transpose(0, 2, 3, 1)
        o4 = _attn_sample(q4, kt_cache, vt_cache, l, knt, vnt, sbias_c, sbias_n)
        oa = o4.transpose(0, 2, 1, 3).reshape(ms, D_A)
        yb, xr, xi = _s5(ub.reshape(bs, ts, D_B), state_ssm_re[l].reshape(bs, D_STATE),
                         state_ssm_im[l].reshape(bs, D_STATE), s5p, lw["wglu"], l, ts)
        yb = yb.reshape(ms, D_MODEL)
        yc, gst = _gla(qc, kc, vc, rc, fc, lw["wfg"], lw["bfg"], lw["gnorm"], l, state_gla[l], bs, ts)
        xs = _post(xs, oa, gates, yb, yc, lw, l, gfin, final, TM)
        outs["kn"].append(knt)
        outs["vn"].append(vnt)
        outs["rs"].append(xr.reshape(bs, G_B, N_B))
        outs["is"].append(xi.reshape(bs, G_B, N_B))
        outs["gs"].append(gst)

    st = {k: jnp.stack(v) for k, v in outs.items()}
    tail = ((0, 0), (0, 0), (0, 0), (0, 0), (LANES - ts, 0))
    kt_new, vt_new = _cache_update(kt_cache, vt_cache, jnp.pad(st["kn"], tail), jnp.pad(st["vn"], tail), ts)
    to_rows = lambda t: jnp.transpose(t, (0, 1, 4, 2, 3))
    return (xp.reshape(bp, sp, D_MODEL), xs.reshape(bs, ts, D_MODEL),
            to_rows(st["kp"]), to_rows(st["vp"]), to_rows(kt_new), to_rows(vt_new),
            st["rp"], st["ip"], st["rs"], st["is"], st["gp"], st["gs"])
```

```python
import functools
import math

import numpy as np
import jax
import jax.numpy as jnp
from jax import lax
from jax.experimental import pallas as pl
from jax.experimental.pallas import tpu as pltpu

F32 = jnp.float32
BF16 = jnp.bfloat16

LANES = 128
D_MODEL = 1024
DEPTH = 4
H_A = 8
HD_A = 64
D_A = H_A * HD_A
N_PAIR = D_A // LANES
PATTERNS = ((128, 1), (512, 4), (2048, 16))
G_A = len(PATTERNS)
N_BACK = 128
W_MAX = 2048
N_BUCKETS = 32
P_B = 16
D_B = 512
G_B = D_B // P_B
N_B = 64
D_STATE = G_B * N_B
H_C = 4
DK_C = 64
DV_C = 128
FG_RANK = 16
FG_PAD = 128
GLA_TAU = 16.0
GLA_CHUNK = 64
D_FF = 4 * D_MODEL
EPS = 1e-6
NEG = -1e30
LOG2E = 1.4426950408889634

SEG_WIDTHS = (G_A * D_A, D_A, D_A, D_B, H_C * DK_C, H_C * DK_C, H_C * DV_C, H_C * DV_C, FG_RANK, 3 * D_MODEL)
SEG_OFFS = tuple(int(v) for v in np.cumsum((0,) + SEG_WIDTHS))
IN_COLS = SEG_OFFS[-1]

VMEM_LIMIT = 56 * 1024 * 1024
TM = 256


def _cparams(sem):
    return pltpu.CompilerParams(dimension_semantics=sem, vmem_limit_bytes=VMEM_LIMIT)


def _resident(shape, layer=None):
    nd = len(shape)
    if layer is None:
        return pl.BlockSpec(shape, lambda *_: (0,) * nd, pipeline_mode=pl.Buffered(1))
    return pl.BlockSpec((None,) + tuple(shape), lambda *_: (layer,) + (0,) * nd, pipeline_mode=pl.Buffered(1))


def _rms(x, g):
    return x * lax.rsqrt(jnp.mean(x * x, axis=-1, keepdims=True) + EPS) * g


def _dot(a, b):
    return jnp.dot(a, b, preferred_element_type=F32)


def _dot_nt(a, b):
    return lax.dot_general(a, b, (((1,), (1,)), ((), ())), preferred_element_type=F32)


def _dot_tn(a, b):
    return lax.dot_general(a, b, (((0,), (0,)), ((), ())), preferred_element_type=F32)


def _in_proj_kernel(x_ref, g_ref, w_ref, *out_refs):
    hb = _rms(x_ref[...], g_ref[...]).astype(BF16)
    for s, o_ref in enumerate(out_refs[:len(SEG_WIDTHS)]):
        z = _dot_nt(hb, w_ref[SEG_OFFS[s]:SEG_OFFS[s + 1], :])
        if s in (0, 4):
            z = z * 0.125
        o_ref[...] = z
        if s in (1, 2) and len(out_refs) > len(SEG_WIDTHS):
            out_refs[len(SEG_WIDTHS) + s - 1][0] = z.T.reshape(H_A, HD_A, z.shape[0])


def _in_proj(x, g, w, layer, tm, kv_t_shape=None):
    m = x.shape[0]
    row = lambda i: (i, 0)
    out_specs = [pl.BlockSpec((tm, wd), row) for wd in SEG_WIDTHS]
    out_shape = [jax.ShapeDtypeStruct((m, wd), F32) for wd in SEG_WIDTHS]
    if kv_t_shape is not None:
        batch, seq = kv_t_shape
        per = seq // tm
        out_specs += [pl.BlockSpec((1, H_A, HD_A, tm), lambda i: (i // per, 0, 0, i % per))] * 2
        out_shape += [jax.ShapeDtypeStruct((batch, H_A, HD_A, seq), F32)] * 2
    return pl.pallas_call(
        _in_proj_kernel,
        grid=(m // tm,),
        in_specs=[pl.BlockSpec((tm, D_MODEL), row), _resident((1, D_MODEL), layer),
                  _resident((IN_COLS, D_MODEL), layer)],
        out_specs=out_specs,
        out_shape=out_shape,
        compiler_params=_cparams(("parallel",)),
        name="in_proj",
    )(x, g, w)


ATTN_UNITS = 8


def _attn_prompt_kernel(q0_ref, q1_ref, q2_ref, k_ref, v_ref, b0_ref, b1_ref, b2_ref, o_ref,
                        qs, k1, v1, k4, v4, k16, v16, tq, tk, tv, og, lg, *, seq):
    nu = seq // N_BACK
    c4 = seq // 4
    left = lax.broadcasted_iota(jnp.int32, (N_BACK, LANES), 1) < HD_A

    def put_q(g, u, q):
        q = q * LOG2E
        qs[g, 2 * N_BACK * u:2 * N_BACK * u + N_BACK, :] = jnp.where(left, q, 0.0).astype(BF16)
        qs[g, 2 * N_BACK * u + N_BACK:2 * N_BACK * (u + 1), :] = jnp.where(left, 0.0, q).astype(BF16)

    @pl.when((pl.program_id(0) == 0) & (pl.program_id(1) == 0))
    def _():
        zpad = jnp.zeros((N_BACK, LANES), BF16)
        for buf in (k1, k4, v1, v4):
            buf[0:N_BACK, 0:LANES] = zpad
        for buf in (v1, v4, v16):
            buf[:, LANES:] = jnp.ones((buf.shape[0], LANES), BF16)

    k1[N_BACK:N_BACK + seq, :] = k_ref[...].astype(BF16)
    v1[N_BACK:N_BACK + seq, 0:LANES] = v_ref[...].astype(BF16)
    for r in range(4):
        by4 = pl.ds(r, c4, stride=4)
        tk[r * c4:(r + 1) * c4, :] = k_ref[by4, :]
        tv[r * c4:(r + 1) * c4, :] = v_ref[by4, :]
        tq[r * c4:(r + 1) * c4, :] = q2_ref[by4, :]
    k4[N_BACK:N_BACK + seq, :] = tk[...].astype(BF16)
    v4[N_BACK:N_BACK + seq, 0:LANES] = tv[...].astype(BF16)
    for u in range(nu):
        put_q(0, u, q0_ref[u * N_BACK:(u + 1) * N_BACK, :])
        put_q(1, u, q1_ref[pl.ds(u // 4 + 4 * N_BACK * (u % 4), N_BACK, stride=4), :])
        by16 = pl.ds((u % 4) * c4 + u // 4, N_BACK, stride=4)
        put_q(2, u, tq[by16, :])
        k16[u * N_BACK:(u + 1) * N_BACK, :] = tk[by16, :].astype(BF16)
        v16[u * N_BACK:(u + 1) * N_BACK, 0:LANES] = tv[by16, :].astype(BF16)

    shp = (2 * N_BACK, 2 * N_BACK)
    qi = lax.broadcasted_iota(jnp.int32, shp, 0) & (N_BACK - 1)
    kk = lax.broadcasted_iota(jnp.int32, shp, 1)
    step = N_BACK + qi - kk
    band = (step >= 0) & (step <= N_BACK)
    tri = band[:, N_BACK:]

    def units(it, carry, *, g, kbuf, vbuf, b_ref, class_blocks, staged):
        def scores(u):
            q = qs[g, pl.ds(pl.multiple_of(u * 2 * N_BACK, 2 * N_BACK), 2 * N_BACK), :]
            k0 = pl.multiple_of(u * N_BACK, N_BACK)
            if class_blocks > 1:
                krows = pl.ds(k0, 2 * N_BACK)
                first = (u & (class_blocks - 1)) == 0
                bias = b_ref[0, pl.ds(pl.multiple_of(jnp.where(first, 2 * N_BACK, 0), 2 * N_BACK), 2 * N_BACK), :]
                valid = band
            else:
                krows = pl.ds(k0, N_BACK)
                bias = b_ref[0, 0:2 * N_BACK, N_BACK:]
                valid = tri
            return k0, krows, jnp.where(valid, _dot_nt(q, kbuf[krows, :]) + bias, NEG)

        def weights(s):
            mx = jnp.max(s, axis=-1, keepdims=True)
            return mx, jnp.exp2(s - mx).astype(BF16)

        def finish(k0, krows, mx, p):
            pv = _dot(p, vbuf[krows, :])
            den = pv[:, LANES:]
            o = pv[:, 0:LANES] / den
            lse = mx + jnp.log2(den)
            og[g, pl.ds(k0, N_BACK), :] = jnp.where(left, o[0:N_BACK], o[N_BACK:])
            lg[g, pl.ds(k0, N_BACK), :] = jnp.where(left, lse[0:N_BACK], lse[N_BACK:])

        us = [it * ATTN_UNITS + i for i in range(ATTN_UNITS)]
        if staged:
            sc = [scores(u) for u in us]
            wt = [weights(s) for _, _, s in sc]
            for (k0, krows, _), (mx, p) in zip(sc, wt):
                finish(k0, krows, mx, p)
        else:
            for u in us:
                k0, krows, s = scores(u)
                finish(k0, krows, *weights(s))
        return carry

    for g, (kbuf, vbuf, b_ref, class_blocks) in enumerate(((k1, v1, b0_ref, nu), (k4, v4, b1_ref, nu // 4),
                                                           (k16, v16, b2_ref, 1))):
        lax.fori_loop(0, nu // ATTN_UNITS, functools.partial(units, g=g, kbuf=kbuf, vbuf=vbuf, b_ref=b_ref,
                                                             class_blocks=class_blocks,
                                                             staged=class_blocks == 1), 0)

    for u in range(nu):
        by16 = pl.ds((u % 4) * c4 + u // 4, N_BACK, stride=4)
        tq[by16, :] = og[2, u * N_BACK:(u + 1) * N_BACK, :]
        tk[by16, :] = lg[2, u * N_BACK:(u + 1) * N_BACK, :]
    for r in range(4):
        for jb in range(c4 // N_BACK):
            rows4 = slice(r * c4 + jb * N_BACK, r * c4 + (jb + 1) * N_BACK)
            tok = pl.ds(r + 4 * N_BACK * jb, N_BACK, stride=4)
            ls = [lg[0, tok, :], lg[1, rows4, :], tk[rows4, :]]
            mx = jnp.maximum(jnp.maximum(ls[0], ls[1]), ls[2])
            ws = [jnp.exp2(l - mx) for l in ls]
            num = ws[0] * og[0, tok, :] + ws[1] * og[1, rows4, :] + ws[2] * tq[rows4, :]
            o_ref[tok, :] = num / (ws[0] + ws[1] + ws[2])


def _attn_prompt(qa, ka, va, biasmats, batch, seq):
    assert tuple(d for _, d in PATTERNS) == (1, 4, 16) and seq == 16 * N_BACK
    blk = (seq, LANES)
    qspec = lambda g: pl.BlockSpec(blk, lambda b, p: (b, g * N_PAIR + p))
    kspec = pl.BlockSpec(blk, lambda b, p: (b, p))
    bspec = pl.BlockSpec((1, 4 * N_BACK, 2 * N_BACK), lambda b, p: (p, 0, 0))
    before = np.arange(2 * N_BACK) < N_BACK
    pair_bias = []
    for bm in biasmats:
        t = (bm * LOG2E).reshape(N_PAIR, 2 * N_BACK, 2 * N_BACK)
        pair_bias.append(jnp.concatenate([t, jnp.where(before, NEG, t)], axis=1))
    act = lambda rows: pltpu.VMEM((rows, LANES), BF16)
    val = lambda rows: pltpu.VMEM((rows, 2 * LANES), BF16)
    return pl.pallas_call(
        functools.partial(_attn_prompt_kernel, seq=seq),
        grid=(batch, N_PAIR),
        in_specs=[qspec(0), qspec(1), qspec(2), kspec, kspec, bspec, bspec, bspec],
        out_specs=kspec,
        out_shape=jax.ShapeDtypeStruct((batch * seq, D_A), F32),
        scratch_shapes=[pltpu.VMEM((G_A, 2 * seq, LANES), BF16),
                        act(N_BACK + seq), val(N_BACK + seq), act(N_BACK + seq), val(N_BACK + seq),
                        act(seq), val(seq),
                        pltpu.VMEM((seq, LANES), F32), pltpu.VMEM((seq, LANES), F32), pltpu.VMEM((seq, LANES), F32),
                        pltpu.VMEM((G_A, seq, LANES), F32), pltpu.VMEM((G_A, seq, LANES), F32)],
        compiler_params=_cparams(("arbitrary", "arbitrary")),
        name="attn_prompt",
    )(qa, qa, qa, ka, va, *pair_bias)


def _attn_sample_kernel(q_ref, kt_ref, vt_ref, knt_ref, vnt_ref, bc_ref, bn_ref, o_ref, *, wbuf, t_new):
    rows = G_A * t_new

    def allowed(shape, key0):
        rr = lax.broadcasted_iota(jnp.int32, shape, 0)
        col = lax.broadcasted_iota(jnp.int32, shape, 1)
        grp = rr >> (t_new.bit_length() - 1)
        dist = wbuf + (rr - grp * t_new) - (col + key0)
        dm1 = jnp.where(grp == 0, PATTERNS[0][1] - 1, jnp.where(grp == 1, PATTERNS[1][1] - 1, PATTERNS[2][1] - 1))
        win = jnp.where(grp == 0, PATTERNS[0][0], jnp.where(grp == 1, PATTERNS[1][0], PATTERNS[2][0]))
        return (dist >= 0) & ((dist & dm1) == 0) & (dist <= win)

    ok_c = allowed((rows, wbuf), 0)
    ok_n = allowed((rows, t_new), wbuf)
    for h in range(H_A):
        qh = q_ref[0, h].astype(BF16)
        sc = jnp.where(ok_c, _dot(qh, kt_ref[0, 0, h].astype(BF16)) + bc_ref[h], NEG)
        sn = jnp.where(ok_n, _dot(qh, knt_ref[0, h].astype(BF16)) + bn_ref[h], NEG)
        mx = jnp.maximum(jnp.max(sc, axis=-1, keepdims=True), jnp.max(sn, axis=-1, keepdims=True))
        mt = jnp.maximum(jnp.maximum(mx[0:t_new], mx[t_new:2 * t_new]), mx[2 * t_new:3 * t_new])
        mall = jnp.concatenate([mt, mt, mt], axis=0)
        pc = jnp.exp(sc - mall)
        pn = jnp.exp(sn - mall)
        den = jnp.sum(pc, axis=-1, keepdims=True) + jnp.sum(pn, axis=-1, keepdims=True)
        acc = (_dot_nt(pc.astype(BF16), vt_ref[0, 0, h].astype(BF16))
               + _dot_nt(pn.astype(BF16), vnt_ref[0, h].astype(BF16)))
        dt = den[0:t_new] + den[t_new:2 * t_new] + den[2 * t_new:3 * t_new]
        at = acc[0:t_new] + acc[t_new:2 * t_new] + acc[2 * t_new:3 * t_new]
        o_ref[0, h] = at / dt


def _attn_sample(q, kt_cache, vt_cache, layer, knt, vnt, bias_c, bias_n):
    batch, wbuf = kt_cache.shape[1], kt_cache.shape[4]
    t_new = knt.shape[3]
    rows = G_A * t_new
    b4 = lambda b: (b, 0, 0, 0)
    cache_spec = pl.BlockSpec((1, 1, H_A, HD_A, wbuf), lambda b: (layer, b, 0, 0, 0))
    new_spec = pl.BlockSpec((1, H_A, HD_A, t_new), b4)
    return pl.pallas_call(
        functools.partial(_attn_sample_kernel, wbuf=wbuf, t_new=t_new),
        grid=(batch,),
        in_specs=[pl.BlockSpec((1, H_A, rows, HD_A), b4), cache_spec, cache_spec, new_spec, new_spec,
                  _resident((H_A, rows, wbuf)), _resident((H_A, rows, t_new))],
        out_specs=pl.BlockSpec((1, H_A, t_new, HD_A), b4),
        out_shape=jax.ShapeDtypeStruct((batch, H_A, t_new, HD_A), F32),
        compiler_params=_cparams(("parallel",)),
        name="attn_sample",
    )(q, kt_cache, vt_cache, knt, vnt, bias_c, bias_n)


def _cache_update_kernel(kt_ref, vt_ref, kn_ref, vn_ref, ko_ref, vo_ref, *, wbuf, t_new):
    is_new = lax.broadcasted_iota(jnp.int32, (HD_A, LANES), 1) >= LANES - t_new
    for src, new, dst in ((kt_ref, kn_ref, ko_ref), (vt_ref, vn_ref, vo_ref)):
        for h in range(H_A):
            moved = pltpu.roll(src[0, 0, h], wbuf - t_new, 1)
            dst[0, 0, h, :, 0:wbuf - LANES] = moved[:, 0:wbuf - LANES]
            dst[0, 0, h, :, wbuf - LANES:wbuf] = jnp.where(is_new, new[0, 0, h], moved[:, wbuf - LANES:wbuf])


def _cache_update(kt_cache, vt_cache, kn_tail, vn_tail, t_new):
    depth, batch, _, _, wbuf = kt_cache.shape
    idx = lambda l, b: (l, b, 0, 0, 0)
    cache_spec = pl.BlockSpec((1, 1, H_A, HD_A, wbuf), idx)
    tail_spec = pl.BlockSpec((1, 1, H_A, HD_A, LANES), idx)
    shape = jax.ShapeDtypeStruct(kt_cache.shape, F32)
    return pl.pallas_call(
        functools.partial(_cache_update_kernel, wbuf=wbuf, t_new=t_new),
        grid=(depth, batch),
        in_specs=[cache_spec, cache_spec, tail_spec, tail_spec],
        out_specs=[cache_spec, cache_spec],
        out_shape=[shape, shape],
        compiler_params=_cparams(("parallel", "parallel")),
        name="cache_update",
    )(kt_cache, vt_cache, kn_tail, vn_tail)


S5_KB = 128
S5_NBLK = D_B // S5_KB
S5_SB = D_STATE // S5_NBLK
S5_TC = 64
S5_SCAN_UNROLL = 8


def _s5_kernel(u_ref, x0r_ref, x0i_ref, ar_ref, ai_ref, bre_ref, bim_ref, cre_ref, cim_ref, d_ref, wglu_ref,
               yb_ref, xr_out, xi_out, ut, ybt, bur, bui, sr, si, *, nb, tc):
    @pl.when(pl.program_id(0) == 0)
    def _():
        sr[...] = x0r_ref[...]
        si[...] = x0i_ref[...]

    for b in range(nb):
        for c in range(D_B // LANES):
            ut[c, pl.ds(b, tc, stride=nb), :] = u_ref[b, :, c * LANES:(c + 1) * LANES]
    u = jnp.concatenate([ut[c] for c in range(D_B // LANES)], axis=1)
    ub = u.astype(BF16)
    for j in range(S5_NBLK):
        uj = ub[:, j * S5_KB:(j + 1) * S5_KB]
        bur[:, j * S5_SB:(j + 1) * S5_SB] = _dot(uj, bre_ref[j])
        bui[:, j * S5_SB:(j + 1) * S5_SB] = _dot(uj, bim_ref[j])

    ar = ar_ref[...]
    ai = ai_ref[...]

    def step(t, carry):
        xr, xi = carry
        rows = pl.ds(pl.multiple_of(t * nb, nb), nb)
        nr = ar * xr - ai * xi + bur[rows, :]
        ni = ar * xi + ai * xr + bui[rows, :]
        bur[rows, :] = nr
        bui[rows, :] = ni
        return nr, ni

    xr, xi = lax.fori_loop(0, tc, step, (sr[...], si[...]), unroll=S5_SCAN_UNROLL)
    sr[...] = xr
    si[...] = xi
    xr_out[...] = xr
    xi_out[...] = xi

    ys = []
    for j in range(S5_NBLK):
        cs = slice(j * S5_SB, (j + 1) * S5_SB)
        ys.append(_dot(bur[:, cs].astype(BF16), cre_ref[j]) - _dot(bui[:, cs].astype(BF16), cim_ref[j]))
    y = jnp.concatenate(ys, axis=-1) + d_ref[...] * u
    ab = _dot(jax.nn.gelu(y).astype(BF16), wglu_ref[...])
    for c in range(D_MODEL // LANES):
        cs = slice(c * LANES, (c + 1) * LANES)
        ybt[c] = ab[:, cs] * jax.nn.sigmoid(ab[:, D_MODEL + c * LANES:D_MODEL + (c + 1) * LANES])
    for b in range(nb):
        for c in range(D_MODEL // LANES):
            yb_ref[b, :, c * LANES:(c + 1) * LANES] = ybt[c, pl.ds(b, tc, stride=nb), :]


def _s5(u, x0r, x0i, prm, wglu, layer, tc):
    nb, seq, _ = u.shape
    r = tc * nb
    blk = lambda i: (0, i, 0)
    st = pl.BlockSpec((nb, D_STATE), lambda i: (0, 0))
    return pl.pallas_call(
        functools.partial(_s5_kernel, nb=nb, tc=tc),
        grid=(seq // tc,),
        in_specs=[pl.BlockSpec((nb, tc, D_B), blk), st, st,
                  _resident((1, D_STATE), layer), _resident((1, D_STATE), layer),
                  _resident((S5_NBLK, S5_KB, S5_SB), layer), _resident((S5_NBLK, S5_KB, S5_SB), layer),
                  _resident((S5_NBLK, S5_SB, S5_KB), layer), _resident((S5_NBLK, S5_SB, S5_KB), layer),
                  _resident((1, D_B), layer), _resident((D_B, 2 * D_MODEL), layer)],
        out_specs=[pl.BlockSpec((nb, tc, D_MODEL), blk), st, st],
        out_shape=[jax.ShapeDtypeStruct((nb, seq, D_MODEL), F32),
                   jax.ShapeDtypeStruct((nb, D_STATE), F32), jax.ShapeDtypeStruct((nb, D_STATE), F32)],
        scratch_shapes=[pltpu.VMEM((D_B // LANES, r, LANES), F32), pltpu.VMEM((D_MODEL // LANES, r, LANES), F32),
                        pltpu.VMEM((r, D_STATE), F32), pltpu.VMEM((r, D_STATE), F32),
                        pltpu.VMEM((nb, D_STATE), F32), pltpu.VMEM((nb, D_STATE), F32)],
        compiler_params=_cparams(("arbitrary",)),
        name="s5",
    )(u, x0r, x0i, prm["ar"], prm["ai"], prm["bre"], prm["bim"], prm["cre"], prm["cim"], prm["d"], wglu)


def _s5_params(log_dt, a_re, a_im, b_re, b_im, c_re, c_im, d_skip):
    depth = log_dt.shape[0]
    dt = jnp.exp(log_dt)[..., None]
    mag = jnp.exp(a_re * dt)
    ab_re = mag * jnp.cos(a_im * dt)
    ab_im = mag * jnp.sin(a_im * dt)
    den = a_re * a_re + a_im * a_im
    nr = ab_re - 1.0
    co_re = (nr * a_re + ab_im * a_im) / den
    co_im = (ab_im * a_re - nr * a_im) / den
    bb_re = co_re[..., None] * b_re - co_im[..., None] * b_im
    bb_im = co_re[..., None] * b_im + co_im[..., None] * b_re
    gpb = G_B // S5_NBLK
    eye = jnp.eye(gpb, dtype=F32)

    def pack_b(bb):
        t = bb.reshape(depth, S5_NBLK, gpb, N_B, P_B)
        m = jnp.einsum('ljgnp,gh->ljgphn', t, eye)
        return m.reshape(depth, S5_NBLK, gpb * P_B, gpb * N_B).astype(BF16)

    def pack_c(cc):
        t = cc.reshape(depth, S5_NBLK, gpb, P_B, N_B)
        m = jnp.einsum('ljgpn,gh->ljgnhp', t, eye)
        return m.reshape(depth, S5_NBLK, gpb * N_B, gpb * P_B).astype(BF16)

    return dict(ar=ab_re.reshape(depth, 1, D_STATE), ai=ab_im.reshape(depth, 1, D_STATE),
                bre=pack_b(bb_re), bim=pack_b(bb_im), cre=pack_c(c_re), cim=pack_c(c_im),
                d=d_skip.reshape(depth, 1, D_B))


def _gla_kernel(q_ref, k_ref, v_ref, r_ref, fc_ref, wfg_ref, bfg_ref, gn_ref, s0_ref, y_ref, sfin_ref, st,
                *, chunk, nchunk, bb):
    tb = pl.program_id(1)

    @pl.when(tb == 0)
    def _():
        for j in range(bb):
            for h in range(H_C):
                st[j * H_C + h] = s0_ref[j, h].T

    rows_t = chunk * nchunk
    sh = chunk.bit_length() - 1
    causal = lax.broadcasted_iota(jnp.int32, (chunk, chunk), 1) <= lax.broadcasted_iota(jnp.int32, (chunk, chunk), 0)
    gn = gn_ref[...]
    groups = [list(range(bb))] if rows_t < GLA_CHUNK else [[j] for j in range(bb)]
    for grp in groups:
        fc = fc_ref[grp[0]] if len(grp) == 1 else jnp.concatenate([fc_ref[j] for j in grp], axis=0)
        rows_g = rows_t * len(grp)
        la = jax.nn.log_sigmoid(_dot(fc.astype(BF16), wfg_ref[...]) + bfg_ref[...]) / GLA_TAU
        ri = lax.broadcasted_iota(jnp.int32, (rows_g, rows_g), 0)
        ci = lax.broadcasted_iota(jnp.int32, (rows_g, rows_g), 1)
        tri = jnp.where(((ri >> sh) == (ci >> sh)) & (ci <= ri), 1.0, 0.0).astype(BF16)
        la_hi = la.astype(BF16)
        la_lo = (la - la_hi.astype(F32)).astype(BF16)
        bcum = _dot(tri, la_hi) + _dot(tri, la_lo)
        for gi, j in enumerate(grp):
            for c in range(nchunk):
                rs = slice(c * chunk, (c + 1) * chunk)
                b = bcum[gi * rows_t + c * chunk:gi * rows_t + (c + 1) * chunk]
                bl = b[chunk - 1:chunk]
                qt = (q_ref[j, rs, :] * jnp.exp(b)).astype(BF16)
                kt = (k_ref[j, rs, :] * jnp.exp(-b)).astype(BF16)
                kd = (k_ref[j, rs, :] * jnp.exp(bl - b)).astype(BF16)
                dec = jnp.exp(bl)
                vb = v_ref[j, rs, :].astype(BF16)
                for h in range(H_C):
                    ks = slice(h * DK_C, (h + 1) * DK_C)
                    vs = slice(h * DV_C, (h + 1) * DV_C)
                    s_t = st[j * H_C + h]
                    att = jnp.where(causal, _dot_nt(qt[:, ks], kt[:, ks]), 0.0)
                    o = _dot(att.astype(BF16), vb[:, vs]) + _dot_nt(qt[:, ks], s_t.astype(BF16))
                    st[j * H_C + h] = s_t * dec[:, ks] + _dot_tn(vb[:, vs], kd[:, ks])
                    y_ref[j, rs, vs] = _rms(o, gn) * jax.nn.silu(r_ref[j, rs, vs])

    @pl.when(tb == pl.num_programs(1) - 1)
    def _():
        for j in range(bb):
            for h in range(H_C):
                sfin_ref[j, h] = st[j * H_C + h].T


GLA_BLOCK_CHUNKS = 4
GLA_BATCH_ROWS = 2


def _gla(qc, kc, vc, rc, fc, wfg, bfg, gnorm, layer, s0, batch, seq):
    chunk = math.gcd(seq, GLA_CHUNK)
    nchunk = min(seq // chunk, GLA_BLOCK_CHUNKS)
    rows_t = chunk * nchunk
    nt = seq // rows_t
    bb = math.gcd(batch, max(1, GLA_CHUNK // rows_t) if nt == 1 else GLA_BATCH_ROWS)
    tok = lambda wd: pl.BlockSpec((bb, rows_t, wd), lambda b, t: (b, t, 0))
    st_spec = pl.BlockSpec((bb, H_C, DK_C, DV_C), lambda b, t: (b, 0, 0, 0))
    by_row = lambda a: a.reshape(batch, seq, a.shape[-1])
    y, s_fin = pl.pallas_call(
        functools.partial(_gla_kernel, chunk=chunk, nchunk=nchunk, bb=bb),
        grid=(batch // bb, nt),
        in_specs=[tok(H_C * DK_C), tok(H_C * DK_C), tok(H_C * DV_C), tok(H_C * DV_C), tok(FG_RANK),
                  _resident((FG_RANK, H_C * DK_C), layer), _resident((1, H_C * DK_C), layer),
                  _resident((1, DV_C), layer), st_spec],
        out_specs=[tok(H_C * DV_C), st_spec],
        out_shape=[jax.ShapeDtypeStruct((batch, seq, H_C * DV_C), F32),
                   jax.ShapeDtypeStruct((batch, H_C, DK_C, DV_C), F32)],
        scratch_shapes=[pltpu.VMEM((bb * H_C, DV_C, DK_C), F32)],
        compiler_params=_cparams(("parallel", "arbitrary")),
        name="gla",
    )(by_row(qc), by_row(kc), by_row(vc), by_row(rc), by_row(fc), wfg, bfg, gnorm, s0)
    return y.reshape(batch * seq, H_C * DV_C), s_fin


def _post_kernel(x_ref, oa_ref, gates_ref, yb_ref, yc_ref, woa_ref, woc_ref, wout_ref, gmlp_ref, wup_ref, wdn_ref,
                 gfin_ref, out_ref, *, final):
    y_a = _dot(oa_ref[...].astype(BF16), woa_ref[...])
    y_c = _dot(yc_ref[...].astype(BF16), woc_ref[...])
    gates = gates_ref[...]
    m = (jax.nn.sigmoid(gates[:, :D_MODEL]) * y_a + jax.nn.sigmoid(gates[:, D_MODEL:2 * D_MODEL]) * yb_ref[...]
         + jax.nn.sigmoid(gates[:, 2 * D_MODEL:]) * y_c)
    x1 = x_ref[...] + _dot(m.astype(BF16), wout_ref[...])
    hm = _rms(x1, gmlp_ref[...]).astype(BF16)
    up = _dot(hm, wup_ref[...])
    act = jnp.square(jnp.maximum(up, 0.0)).astype(BF16)
    x2 = x1 + _dot(act, wdn_ref[...])
    out_ref[...] = _rms(x2, gfin_ref[...]) if final else x2


def _post(x, oa, gates, yb, yc, lw, layer, gfin, final, tm):
    m = x.shape[0]
    row = lambda i: (i, 0)
    tok = lambda wd: pl.BlockSpec((tm, wd), row)
    return pl.pallas_call(
        functools.partial(_post_kernel, final=final),
        grid=(m // tm,),
        in_specs=[tok(D_MODEL), tok(D_A), tok(3 * D_MODEL), tok(D_MODEL), tok(H_C * DV_C),
                  _resident((D_A, D_MODEL), layer), _resident((H_C * DV_C, D_MODEL), layer),
                  _resident((D_MODEL, D_MODEL), layer), _resident((1, D_MODEL), layer),
                  _resident((D_MODEL, D_FF), layer), _resident((D_FF, D_MODEL), layer),
                  _resident((1, D_MODEL))],
        out_specs=tok(D_MODEL),
        out_shape=jax.ShapeDtypeStruct((m, D_MODEL), F32),
        compiler_params=_cparams(("parallel",)),
        name="post",
    )(x, oa, gates, yb, yc, lw["woa"], lw["woc"], lw["wout"], lw["gmlp"], lw["wup"], lw["wdn"], gfin)


def _t5_bucket(dist):
    exact = N_BUCKETS // 2
    d = np.maximum(dist, 1).astype(np.float32)
    large = exact + (np.log(d / exact) / np.log(W_MAX / exact) * (N_BUCKETS - exact)).astype(np.int32)
    large = np.minimum(large, N_BUCKETS - 1)
    return np.where(dist < exact, dist, large).astype(np.int32)


def _toeplitz(u, rows, cols, off):
    period = cols + off + 1
    w = jnp.pad(u, ((0, 0), (0, period - u.shape[1])))
    a = jnp.tile(w, (1, rows))[:, :rows * (period - 1)].reshape(u.shape[0], rows, period - 1)
    return a[:, :, off:off + cols]


def _prompt_bias(rel_bias, g, dil):
    x = np.arange(3 * N_BACK - 1)
    bk = _t5_bucket(np.clip(2 * N_BACK - 1 - x, 0, N_BACK) * dil)
    u = rel_bias[bk][:, g * H_A:(g + 1) * H_A].T
    return _toeplitz(u, N_BACK, 2 * N_BACK, N_BACK - 1)


def _sample_bias(rel_bias, wbuf, t_new):
    n = wbuf + t_new
    x = np.arange(n + t_new - 1)
    bk = _t5_bucket(np.clip(n - 1 - x, 0, W_MAX))
    tab = rel_bias[bk]
    per_g = [_toeplitz(tab[:, g * H_A:(g + 1) * H_A].T, t_new, n, t_new - 1) for g in range(G_A)]
    full = jnp.concatenate(per_g, axis=1)
    return full[:, :, :wbuf], full[:, :, wbuf:]


def _weights(w_in, w_o_a, w_glu, w_fg2, b_fg, gla_norm, w_o_c, w_out, norm_mix, norm_mlp, w_up, w_down):
    depth = w_in.shape[0]
    return dict(
        win=jnp.transpose(w_in, (0, 2, 1)).astype(BF16),
        gmix=norm_mix.reshape(depth, 1, D_MODEL),
        wglu=w_glu.astype(BF16),
        wfg=w_fg2.astype(BF16),
        bfg=b_fg.reshape(depth, 1, H_C * DK_C),
        gnorm=gla_norm.reshape(depth, 1, DV_C),
        woa=w_o_a.astype(BF16), woc=w_o_c.astype(BF16), wout=w_out.astype(BF16),
        gmlp=norm_mlp.reshape(depth, 1, D_MODEL),
        wup=w_up.astype(BF16), wdn=w_down.astype(BF16),
    )


def kernel(x_prompt, x_sample, cache_k_win, cache_v_win, state_ssm_re, state_ssm_im, state_gla, rel_bias, norm_mix, w_in, w_o_a, s5_log_dt, s5_a_re, s5_a_im, s5_b_re, s5_b_im, s5_c_re, s5_c_im, s5_d, w_glu, w_fg2, b_fg, gla_norm, w_o_c, w_out, norm_mlp, w_up, w_down, norm_final):
    bp, sp, _ = x_prompt.shape
    bs, ts, _ = x_sample.shape
    wbuf = cache_k_win.shape[2]
    mp, ms = bp * sp, bs * ts
    gfin = norm_final.reshape(1, D_MODEL)
    kt_cache = jnp.transpose(cache_k_win, (0, 1, 3, 4, 2))
    vt_cache = jnp.transpose(cache_v_win, (0, 1, 3, 4, 2))
    pbias = [_prompt_bias(rel_bias, g, dil) for g, (_, dil) in enumerate(PATTERNS)]
    sbias_c, sbias_n = _sample_bias(rel_bias, wbuf, ts)

    xp = x_prompt.reshape(mp, D_MODEL)
    xs = x_sample.reshape(ms, D_MODEL)
    outs = {k: [] for k in ("kp", "vp", "kn", "vn", "rp", "ip", "rs", "is", "gp", "gs")}
    lw = _weights(w_in, w_o_a, w_glu, w_fg2, b_fg, gla_norm, w_o_c, w_out, norm_mix, norm_mlp, w_up, w_down)
    s5p = _s5_params(s5_log_dt, s5_a_re, s5_a_im, s5_b_re, s5_b_im, s5_c_re, s5_c_im, s5_d)
    assert sp <= W_MAX
    for l in range(DEPTH):
        final = l == DEPTH - 1

        qa, ka, va, ub, qc, kc, vc, rc, fc, gates, kt, vt = _in_proj(xp, lw["gmix"], lw["win"], l, TM, (bp, sp))
        oa = _attn_prompt(qa, ka, va, pbias, bp, sp)
        zeros = jnp.zeros((bp, D_STATE), F32)
        yb, xr, xi = _s5(ub.reshape(bp, sp, D_B), zeros, zeros, s5p, lw["wglu"], l, S5_TC)
        yb = yb.reshape(mp, D_MODEL)
        yc, gst = _gla(qc, kc, vc, rc, fc, lw["wfg"], lw["bfg"], lw["gnorm"], l,
                       jnp.zeros((bp, H_C, DK_C, DV_C), F32), bp, sp)
        xp = _post(xp, oa, gates, yb, yc, lw, l, gfin, final, TM)
        outs["kp"].append(kt)
        outs["vp"].append(vt)
        outs["rp"].append(xr.reshape(bp, G_B, N_B))
        outs["ip"].append(xi.reshape(bp, G_B, N_B))
        outs["gp"].append(gst)

        qa, ka, va, ub, qc, kc, vc, rc, fc, gates = _in_proj(xs, lw["gmix"], lw["win"], l, TM)
        q4 =qa.reshape(bs, ts, G_A, H_A, HD_A).transpose(0, 3, 2, 1, 4).reshape(bs, H_A, G_A * ts, HD_A)
        knt = ka.reshape(bs, ts, H_A, HD_A).transpose(0, 2, 3, 1)
        vnt = va.reshape(bs, ts, H_A, HD_A).transpose(0, 2, 3, 1)
        o4 = _attn_sample(q4, kt_cache, vt_cache, l, knt, vnt, sbias_c, sbias_n)
        oa = o4.transpose(0, 2, 1, 3).reshape(ms, D_A)
        yb, xr, xi = _s5(ub.reshape(bs, ts, D_B), state_ssm_re[l].reshape(bs, D_STATE),
                         state_ssm_im[l].reshape(bs, D_STATE), s5p, lw["wglu"], l, ts)
        yb = yb.reshape(ms, D_MODEL)
        yc, gst = _gla(qc, kc, vc, rc, fc, lw["wfg"], lw["bfg"], lw["gnorm"], l, state_gla[l], bs, ts)
        xs = _post(xs, oa, gates, yb, yc, lw, l, gfin, final, TM)
        outs["kn"].append(knt)
        outs["vn"].append(vnt)
        outs["rs"].append(xr.reshape(bs, G_B, N_B))
        outs["is"].append(xi.reshape(bs, G_B, N_B))
        outs["gs"].append(gst)

    st = {k: jnp.stack(v) for k, v in outs.items()}
    tail = ((0, 0), (0, 0), (0, 0), (0, 0), (LANES - ts, 0))
    kt_new, vt_new = _cache_update(kt_cache, vt_cache, jnp.pad(st["kn"], tail), jnp.pad(st["vn"], tail), ts)
    to_rows = lambda t: jnp.transpose(t, (0, 1, 4, 2, 3))
    return (xp.reshape(bp, sp, D_MODEL), xs.reshape(bs, ts, D_MODEL),
            to_rows(st["kp"]), to_rows(st["vp"]), to_rows(kt_new), to_rows(vt_new),
            st["rp"], st["ip"], st["rs"], st["is"], st["gp"], st["gs"])
```

```python
import functools
import math

import numpy as np
import jax
import jax.numpy as jnp
from jax import lax
from jax.experimental import pallas as pl
from jax.experimental.pallas import tpu as pltpu

F32 = jnp.float32
BF16 = jnp.bfloat16

LANES = 128
D_MODEL = 1024
DEPTH = 4
H_A = 8
HD_A = 64
D_A = H_A * HD_A
N_PAIR = D_A // LANES
PATTERNS = ((128, 1), (512, 4), (2048, 16))
G_A = len(PATTERNS)
N_BACK = 128
W_MAX = 2048
N_BUCKETS = 32
P_B = 16
D_B = 512
G_B = D_B // P_B
N_B = 64
D_STATE = G_B * N_B
H_C = 4
DK_C = 64
DV_C = 128
FG_RANK = 16
FG_PAD = 128
GLA_TAU = 16.0
GLA_CHUNK = 64
D_FF = 4 * D_MODEL
EPS = 1e-6
NEG = -1e30
LOG2E = 1.4426950408889634

SEG_WIDTHS = (G_A * D_A, D_A, D_A, D_B, H_C * DK_C, H_C * DK_C, H_C * DV_C, H_C * DV_C, FG_RANK, 3 * D_MODEL)
SEG_OFFS = tuple(int(v) for v in np.cumsum((0,) + SEG_WIDTHS))
IN_COLS = SEG_OFFS[-1]

VMEM_LIMIT = 56 * 1024 * 1024
TM = 256


def _cparams(sem):
    return pltpu.CompilerParams(dimension_semantics=sem, vmem_limit_bytes=VMEM_LIMIT)


def _resident(shape, layer=None):
    nd = len(shape)
    if layer is None:
        return pl.BlockSpec(shape, lambda *_: (0,) * nd, pipeline_mode=pl.Buffered(1))
    return pl.BlockSpec((None,) + tuple(shape), lambda *_: (layer,) + (0,) * nd, pipeline_mode=pl.Buffered(1))


def _rms(x, g):
    return x * lax.rsqrt(jnp.mean(x * x, axis=-1, keepdims=True) + EPS) * g


def _dot(a, b):
    return jnp.dot(a, b, preferred_element_type=F32)


def _dot_nt(a, b):
    return lax.dot_general(a, b, (((1,), (1,)), ((), ())), preferred_element_type=F32)


def _dot_tn(a, b):
    return lax.dot_general(a, b, (((0,), (0,)), ((), ())), preferred_element_type=F32)


def _in_proj_kernel(x_ref, g_ref, w_ref, *out_refs):
    hb = _rms(x_ref[...], g_ref[...]).astype(BF16)
    for s, o_ref in enumerate(out_refs[:len(SEG_WIDTHS)]):
        z = _dot_nt(hb, w_ref[SEG_OFFS[s]:SEG_OFFS[s + 1], :])
        if s in (0, 4):
            z = z * 0.125
        o_ref[...] = z
        if s in (1, 2) and len(out_refs) > len(SEG_WIDTHS):
            out_refs[len(SEG_WIDTHS) + s - 1][0] = z.T.reshape(H_A, HD_A, z.shape[0])


def _in_proj(x, g, w, layer, tm, kv_t_shape=None):
    m = x.shape[0]
    row = lambda i: (i, 0)
    out_specs = [pl.BlockSpec((tm, wd), row) for wd in SEG_WIDTHS]
    out_shape = [jax.ShapeDtypeStruct((m, wd), F32) for wd in SEG_WIDTHS]
    if kv_t_shape is not None:
        batch, seq = kv_t_shape
        per = seq // tm
        out_specs += [pl.BlockSpec((1, H_A, HD_A, tm), lambda i: (i // per, 0, 0, i % per))] * 2
        out_shape += [jax.ShapeDtypeStruct((batch, H_A, HD_A, seq), F32)] * 2
    return pl.pallas_call(
        _in_proj_kernel,
        grid=(m // tm,),
        in_specs=[pl.BlockSpec((tm, D_MODEL), row), _resident((1, D_MODEL), layer),
                  _resident((IN_COLS, D_MODEL), layer)],
        out_specs=out_specs,
        out_shape=out_shape,
        compiler_params=_cparams(("parallel",)),
        name="in_proj",
    )(x, g, w)


ATTN_UNITS = 8


def _attn_prompt_kernel(q0_ref, q1_ref, q2_ref, k_ref, v_ref, b0_ref, b1_ref, b2_ref, o_ref,
                        qs, k1, v1, k4, v4, k16, v16, tq, tk, tv, og, lg, *, seq):
    nu = seq // N_BACK
    c4 = seq // 4
    left = lax.broadcasted_iota(jnp.int32, (N_BACK, LANES), 1) < HD_A

    def put_q(g, u, q):
        q = q * LOG2E
        qs[g, 2 * N_BACK * u:2 * N_BACK * u + N_BACK, :] = jnp.where(left, q, 0.0).astype(BF16)
        qs[g, 2 * N_BACK * u + N_BACK:2 * N_BACK * (u + 1), :] = jnp.where(left, 0.0, q).astype(BF16)

    @pl.when((pl.program_id(0) == 0) & (pl.program_id(1) == 0))
    def _():
        zpad = jnp.zeros((N_BACK, LANES), BF16)
        for buf in (k1, k4, v1, v4):
            buf[0:N_BACK, 0:LANES] = zpad
        for buf in (v1, v4, v16):
            buf[:, LANES:] = jnp.ones((buf.shape[0], LANES), BF16)

    k1[N_BACK:N_BACK + seq, :] = k_ref[...].astype(BF16)
    v1[N_BACK:N_BACK + seq, 0:LANES] = v_ref[...].astype(BF16)
    for r in range(4):
        by4 = pl.ds(r, c4, stride=4)
        tk[r * c4:(r + 1) * c4, :] = k_ref[by4, :]
        tv[r * c4:(r + 1) * c4, :] = v_ref[by4, :]
        tq[r * c4:(r + 1) * c4, :] = q2_ref[by4, :]
    k4[N_BACK:N_BACK + seq, :] = tk[...].astype(BF16)
    v4[N_BACK:N_BACK + seq, 0:LANES] = tv[...].astype(BF16)
    for u in range(nu):
        put_q(0, u, q0_ref[u * N_BACK:(u + 1) * N_BACK, :])
        put_q(1, u, q1_ref[pl.ds(u // 4 + 4 * N_BACK * (u % 4), N_BACK, stride=4), :])
        by16 = pl.ds((u % 4) * c4 + u // 4, N_BACK, stride=4)
        put_q(2, u, tq[by16, :])
        k16[u * N_BACK:(u + 1) * N_BACK, :] = tk[by16, :].astype(BF16)
        v16[u * N_BACK:(u + 1) * N_BACK, 0:LANES] = tv[by16, :].astype(BF16)

    shp = (2 * N_BACK, 2 * N_BACK)
    qi = lax.broadcasted_iota(jnp.int32, shp, 0) & (N_BACK - 1)
    kk = lax.broadcasted_iota(jnp.int32, shp, 1)
    step = N_BACK + qi - kk
    band = (step >= 0) & (step <= N_BACK)
    tri = band[:, N_BACK:]

    def units(it, carry, *, g, kbuf, vbuf, b_ref, class_blocks, staged):
        def scores(u):
            q = qs[g, pl.ds(pl.multiple_of(u * 2 * N_BACK, 2 * N_BACK), 2 * N_BACK), :]
            k0 = pl.multiple_of(u * N_BACK, N_BACK)
            if class_blocks > 1:
                krows = pl.ds(k0, 2 * N_BACK)
                first = (u & (class_blocks - 1)) == 0
                bias = b_ref[0, pl.ds(pl.multiple_of(jnp.where(first, 2 * N_BACK, 0), 2 * N_BACK), 2 * N_BACK), :]
                valid = band
            else:
                krows = pl.ds(k0, N_BACK)
                bias = b_ref[0, 0:2 * N_BACK, N_BACK:]
                valid = tri
            return k0, krows, jnp.where(valid, _dot_nt(q, kbuf[krows, :]) + bias, NEG)

        def weights(s):
            mx = jnp.max(s, axis=-1, keepdims=True)
            return mx, jnp.exp2(s - mx).astype(BF16)

        def finish(k0, krows, mx, p):
            pv = _dot(p, vbuf[krows, :])
            den = pv[:, LANES:]
            o = pv[:, 0:LANES] / den
            lse = mx + jnp.log2(den)
            og[g, pl.ds(k0, N_BACK), :] = jnp.where(left, o[0:N_BACK], o[N_BACK:])
            lg[g, pl.ds(k0, N_BACK), :] = jnp.where(left, lse[0:N_BACK], lse[N_BACK:])

        us = [it * ATTN_UNITS + i for i in range(ATTN_UNITS)]
        if staged:
            sc = [scores(u) for u in us]
            wt = [weights(s) for _, _, s in sc]
            for (k0, krows, _), (mx, p) in zip(sc, wt):
                finish(k0, krows, mx, p)
        else:
            for u in us:
                k0, krows, s = scores(u)
                finish(k0, krows, *weights(s))
        return carry

    for g, (kbuf, vbuf, b_ref, class_blocks) in enumerate(((k1, v1, b0_ref, nu), (k4, v4, b1_ref, nu // 4),
                                                           (k16, v16, b2_ref, 1))):
        lax.fori_loop(0, nu // ATTN_UNITS, functools.partial(units, g=g, kbuf=kbuf, vbuf=vbuf, b_ref=b_ref,
                                                             class_blocks=class_blocks,
                                                             staged=class_blocks == 1), 0)

    for u in range(nu):
        by16 = pl.ds((u % 4) * c4 + u // 4, N_BACK, stride=4)
        tq[by16, :] = og[2, u * N_BACK:(u + 1) * N_BACK, :]
        tk[by16, :] = lg[2, u * N_BACK:(u + 1) * N_BACK, :]
    for r in range(4):
        for jb in range(c4 // N_BACK):
            rows4 = slice(r * c4 + jb * N_BACK, r * c4 + (jb + 1) * N_BACK)
            tok = pl.ds(r + 4 * N_BACK * jb, N_BACK, stride=4)
            ls = [lg[0, tok, :], lg[1, rows4, :], tk[rows4, :]]
            mx = jnp.maximum(jnp.maximum(ls[0], ls[1]), ls[2])
            ws = [jnp.exp2(l - mx) for l in ls]
            num = ws[0] * og[0, tok, :] + ws[1] * og[1, rows4, :] + ws[2] * tq[rows4, :]
            o_ref[tok, :] = num / (ws[0] + ws[1] + ws[2])


def _attn_prompt(qa, ka, va, biasmats, batch, seq):
    assert tuple(d for _, d in PATTERNS) == (1, 4, 16) and seq == 16 * N_BACK
    blk = (seq, LANES)
    qspec = lambda g: pl.BlockSpec(blk, lambda b, p: (b, g * N_PAIR + p))
    kspec = pl.BlockSpec(blk, lambda b, p: (b, p))
    bspec = pl.BlockSpec((1, 4 * N_BACK, 2 * N_BACK), lambda b, p: (p, 0, 0))
    before = np.arange(2 * N_BACK) < N_BACK
    pair_bias = []
    for bm in biasmats:
        t = (bm * LOG2E).reshape(N_PAIR, 2 * N_BACK, 2 * N_BACK)
        pair_bias.append(jnp.concatenate([t, jnp.where(before, NEG, t)], axis=1))
    act = lambda rows: pltpu.VMEM((rows, LANES), BF16)
    val = lambda rows: pltpu.VMEM((rows, 2 * LANES), BF16)
    return pl.pallas_call(
        functools.partial(_attn_prompt_kernel, seq=seq),
        grid=(batch, N_PAIR),
        in_specs=[qspec(0), qspec(1), qspec(2), kspec, kspec, bspec, bspec, bspec],
        out_specs=kspec,
        out_shape=jax.ShapeDtypeStruct((batch * seq, D_A), F32),
        scratch_shapes=[pltpu.VMEM((G_A, 2 * seq, LANES), BF16),
                        act(N_BACK + seq), val(N_BACK + seq), act(N_BACK + seq), val(N_BACK + seq),
                        act(seq), val(seq),
                        pltpu.VMEM((seq, LANES), F32), pltpu.VMEM((seq, LANES), F32), pltpu.VMEM((seq, LANES), F32),
                        pltpu.VMEM((G_A, seq, LANES), F32), pltpu.VMEM((G_A, seq, LANES), F32)],
        compiler_params=_cparams(("arbitrary", "arbitrary")),
        name="attn_prompt",
    )(qa, qa, qa, ka, va, *pair_bias)


def _attn_sample_kernel(q_ref, kt_ref, vt_ref, knt_ref, vnt_ref, bc_ref, bn_ref, o_ref, *, wbuf, t_new):
    rows = G_A * t_new

    def allowed(shape, key0):
        rr = lax.broadcasted_iota(jnp.int32, shape, 0)
        col = lax.broadcasted_iota(jnp.int32, shape, 1)
        grp = rr >> (t_new.bit_length() - 1)
        dist = wbuf + (rr - grp * t_new) - (col + key0)
        dm1 = jnp.where(grp == 0, PATTERNS[0][1] - 1, jnp.where(grp == 1, PATTERNS[1][1] - 1, PATTERNS[2][1] - 1))
        win = jnp.where(grp == 0, PATTERNS[0][0], jnp.where(grp == 1, PATTERNS[1][0], PATTERNS[2][0]))
        return (dist >= 0) & ((dist & dm1) == 0) & (dist <= win)

    ok_c = allowed((rows, wbuf), 0)
    ok_n = allowed((rows, t_new), wbuf)
    for h in range(H_A):
        qh = q_ref[0, h].astype(BF16)
        sc = jnp.where(ok_c, _dot(qh, kt_ref[0, 0, h].astype(BF16)) + bc_ref[h], NEG)
        sn = jnp.where(ok_n, _dot(qh, knt_ref[0, h].astype(BF16)) + bn_ref[h], NEG)
        mx = jnp.maximum(jnp.max(sc, axis=-1, keepdims=True), jnp.max(sn, axis=-1, keepdims=True))
        mt = jnp.maximum(jnp.maximum(mx[0:t_new], mx[t_new:2 * t_new]), mx[2 * t_new:3 * t_new])
        mall = jnp.concatenate([mt, mt, mt], axis=0)
        pc = jnp.exp(sc - mall)
        pn = jnp.exp(sn - mall)
        den = jnp.sum(pc, axis=-1, keepdims=True) + jnp.sum(pn, axis=-1, keepdims=True)
        acc = (_dot_nt(pc.astype(BF16), vt_ref[0, 0, h].astype(BF16))
               + _dot_nt(pn.astype(BF16), vnt_ref[0, h].astype(BF16)))
        dt = den[0:t_new] + den[t_new:2 * t_new] + den[2 * t_new:3 * t_new]
        at = acc[0:t_new] + acc[t_new:2 * t_new] + acc[2 * t_new:3 * t_new]
        o_ref[0, h] = at / dt


def _attn_sample(q, kt_cache, vt_cache, layer, knt, vnt, bias_c, bias_n):
    batch, wbuf = kt_cache.shape[1], kt_cache.shape[4]
    t_new = knt.shape[3]
    rows = G_A * t_new
    b4 = lambda b: (b, 0, 0, 0)
    cache_spec = pl.BlockSpec((1, 1, H_A, HD_A, wbuf), lambda b: (layer, b, 0, 0, 0))
    new_spec = pl.BlockSpec((1, H_A, HD_A, t_new), b4)
    return pl.pallas_call(
        functools.partial(_attn_sample_kernel, wbuf=wbuf, t_new=t_new),
        grid=(batch,),
        in_specs=[pl.BlockSpec((1, H_A, rows, HD_A), b4), cache_spec, cache_spec, new_spec, new_spec,
                  _resident((H_A, rows, wbuf)), _resident((H_A, rows, t_new))],
        out_specs=pl.BlockSpec((1, H_A, t_new, HD_A), b4),
        out_shape=jax.ShapeDtypeStruct((batch, H_A, t_new, HD_A), F32),
        compiler_params=_cparams(("parallel",)),
        name="attn_sample",
    )(q, kt_cache, vt_cache, knt, vnt, bias_c, bias_n)


def _cache_update_kernel(kt_ref, vt_ref, kn_ref, vn_ref, ko_ref, vo_ref, *, wbuf, t_new):
    is_new = lax.broadcasted_iota(jnp.int32, (HD_A, LANES), 1) >= LANES - t_new
    for src, new, dst in ((kt_ref, kn_ref, ko_ref), (vt_ref, vn_ref, vo_ref)):
        for h in range(H_A):
            moved = pltpu.roll(src[0, 0, h], wbuf - t_new, 1)
            dst[0, 0, h, :, 0:wbuf - LANES] = moved[:, 0:wbuf - LANES]
            dst[0, 0, h, :, wbuf - LANES:wbuf] = jnp.where(is_new, new[0, 0, h], moved[:, wbuf - LANES:wbuf])


def _cache_update(kt_cache, vt_cache, kn_tail, vn_tail, t_new):
    depth, batch, _, _, wbuf = kt_cache.shape
    idx = lambda l, b: (l, b, 0, 0, 0)
    cache_spec = pl.BlockSpec((1, 1, H_A, HD_A, wbuf), idx)
    tail_spec = pl.BlockSpec((1, 1, H_A, HD_A, LANES), idx)
    shape = jax.ShapeDtypeStruct(kt_cache.shape, F32)
    return pl.pallas_call(
        functools.partial(_cache_update_kernel, wbuf=wbuf, t_new=t_new),
        grid=(depth, batch),
        in_specs=[cache_spec, cache_spec, tail_spec, tail_spec],
        out_specs=[cache_spec, cache_spec],
        out_shape=[shape, shape],
        compiler_params=_cparams(("parallel", "parallel")),
        name="cache_update",
    )(kt_cache, vt_cache, kn_tail, vn_tail)


S5_KB = 128
S5_NBLK = D_B // S5_KB
S5_SB = D_STATE // S5_NBLK
S5_TC = 128


S5_SCAN_UNROLL = 8


def _s5_kernel(u_ref, x0r_ref, x0i_ref, ar_ref, ai_ref, bre_ref, bim_ref, cre_ref, cim_ref, d_ref, wglu_ref,
               yb_ref, xr_out, xi_out, ut, ybt, bur, bui, sr, si, *, nb, tc):
    @pl.when(pl.program_id(0) == 0)
    def _():
        sr[...] = x0r_ref[...]
        si[...] = x0i_ref[...]

    for b in range(nb):
        for c in range(D_B // LANES):
            ut[c, pl.ds(b, tc, stride=nb), :] = u_ref[b, :, c * LANES:(c + 1) * LANES]
    u = jnp.concatenate([ut[c] for c in range(D_B // LANES)], axis=1)
    ub = u.astype(BF16)
    for j in range(S5_NBLK):
        uj = ub[:, j * S5_KB:(j + 1) * S5_KB]
        bur[:, j * S5_SB:(j + 1) * S5_SB] = _dot(uj, bre_ref[j])
        bui[:, j * S5_SB:(j + 1) * S5_SB] = _dot(uj, bim_ref[j])

    ar = ar_ref[...]
    ai = ai_ref[...]

    def step(t, carry):
        xr, xi = carry
        rows = pl.ds(pl.multiple_of(t * nb, nb), nb)
        nr = ar * xr - ai * xi + bur[rows, :]
        ni = ar * xi + ai * xr + bui[rows, :]
        bur[rows, :] = nr
        bui[rows, :] = ni
        return nr, ni

    xr, xi = lax.fori_loop(0, tc, step, (sr[...], si[...]), unroll=S5_SCAN_UNROLL)
    sr[...] = xr
    si[...] = xi
    xr_out[...] = xr
    xi_out[...] = xi

    ys = []
    for j in range(S5_NBLK):
        cs = slice(j * S5_SB, (j + 1) * S5_SB)
        ys.append(_dot(bur[:, cs].astype(BF16), cre_ref[j]) - _dot(bui[:, cs].astype(BF16), cim_ref[j]))
    y = jnp.concatenate(ys, axis=-1) + d_ref[...] * u
    ab = _dot(jax.nn.gelu(y).astype(BF16), wglu_ref[...])
    for c in range(D_MODEL // LANES):
        cs = slice(c * LANES, (c + 1) * LANES)
        ybt[c] = ab[:, cs] * jax.nn.sigmoid(ab[:, D_MODEL + c * LANES:D_MODEL + (c + 1) * LANES])
    for b in range(nb):
        for c in range(D_MODEL // LANES):
            yb_ref[b, :, c * LANES:(c + 1) * LANES] = ybt[c, pl.ds(b, tc, stride=nb), :]


def _s5(u, x0r, x0i, prm, wglu, layer, tc):
    nb, seq, _ = u.shape
    r = tc * nb
    blk = lambda i: (0, i, 0)
    st = pl.BlockSpec((nb, D_STATE), lambda i: (0, 0))
    return pl.pallas_call(
        functools.partial(_s5_kernel, nb=nb, tc=tc),
        grid=(seq // tc,),
        in_specs=[pl.BlockSpec((nb, tc, D_B), blk), st, st,
                  _resident((1, D_STATE), layer), _resident((1, D_STATE), layer),
                  _resident((S5_NBLK, S5_KB, S5_SB), layer), _resident((S5_NBLK, S5_KB, S5_SB), layer),
                  _resident((S5_NBLK, S5_SB, S5_KB), layer), _resident((S5_NBLK, S5_SB, S5_KB), layer),
                  _resident((1, D_B), layer), _resident((D_B, 2 * D_MODEL), layer)],
        out_specs=[pl.BlockSpec((nb, tc, D_MODEL), blk), st, st],
        out_shape=[jax.ShapeDtypeStruct((nb, seq, D_MODEL), F32),
                   jax.ShapeDtypeStruct((nb, D_STATE), F32), jax.ShapeDtypeStruct((nb, D_STATE), F32)],
        scratch_shapes=[pltpu.VMEM((D_B // LANES, r, LANES), F32), pltpu.VMEM((D_MODEL // LANES, r, LANES), F32),
                        pltpu.VMEM((r, D_STATE), F32), pltpu.VMEM((r, D_STATE), F32),
                        pltpu.VMEM((nb, D_STATE), F32), pltpu.VMEM((nb, D_STATE), F32)],
        compiler_params=_cparams(("arbitrary",)),
        name="s5",
    )(u, x0r, x0i, prm["ar"], prm["ai"], prm["bre"], prm["bim"], prm["cre"], prm["cim"], prm["d"], wglu)


def _s5_params(log_dt, a_re, a_im, b_re, b_im, c_re, c_im, d_skip):
    depth = log_dt.shape[0]
    dt = jnp.exp(log_dt)[..., None]
    mag = jnp.exp(a_re * dt)
    ab_re = mag * jnp.cos(a_im * dt)
    ab_im = mag * jnp.sin(a_im * dt)
    den = a_re * a_re + a_im * a_im
    nr = ab_re - 1.0
    co_re = (nr * a_re + ab_im * a_im) / den
    co_im = (ab_im * a_re - nr * a_im) / den
    bb_re = co_re[..., None] * b_re - co_im[..., None] * b_im
    bb_im = co_re[..., None] * b_im + co_im[..., None] * b_re
    gpb = G_B // S5_NBLK
    eye = jnp.eye(gpb, dtype=F32)

    def pack_b(bb):
        t = bb.reshape(depth, S5_NBLK, gpb, N_B, P_B)
        m = jnp.einsum('ljgnp,gh->ljgphn', t, eye)
        return m.reshape(depth, S5_NBLK, gpb * P_B, gpb * N_B).astype(BF16)

    def pack_c(cc):
        t = cc.reshape(depth, S5_NBLK, gpb, P_B, N_B)
        m = jnp.einsum('ljgpn,gh->ljgnhp', t, eye)
        return m.reshape(depth, S5_NBLK, gpb * N_B, gpb * P_B).astype(BF16)

    return dict(ar=ab_re.reshape(depth, 1, D_STATE), ai=ab_im.reshape(depth, 1, D_STATE),
                bre=pack_b(bb_re), bim=pack_b(bb_im), cre=pack_c(c_re), cim=pack_c(c_im),
                d=d_skip.reshape(depth, 1, D_B))


def _gla_kernel(q_ref, k_ref, v_ref, r_ref, fc_ref, wfg_ref, bfg_ref, gn_ref, s0_ref, y_ref, sfin_ref, st,
                *, chunk, nchunk, bb):
    tb = pl.program_id(1)
    dk_all, dv_all = H_C * DK_C, H_C * DV_C

    def head_of(shape, dim, width):
        return lax.broadcasted_iota(jnp.int32, shape, dim) >> (width.bit_length() - 1)

    tok_all = H_C * chunk
    diag_k = head_of((tok_all, dk_all), 0, chunk) == head_of((tok_all, dk_all), 1, DK_C)
    diag_v = head_of((tok_all, dv_all), 0, chunk) == head_of((tok_all, dv_all), 1, DV_C)

    @pl.when(tb == 0)
    def _():
        for j in range(bb):
            for h in range(H_C):
                st[j * H_C + h] = s0_ref[j, h].T

    rows_t = chunk * nchunk
    sh = chunk.bit_length() - 1
    col_tok = lax.broadcasted_iota(jnp.int32, (chunk, tok_all), 1) & (chunk - 1)
    causal = col_tok <= lax.broadcasted_iota(jnp.int32, (chunk, tok_all), 0)
    gn = gn_ref[...]
    groups = [list(range(bb))] if rows_t < GLA_CHUNK else [[j] for j in range(bb)]
    for grp in groups:
        fc = fc_ref[grp[0]] if len(grp) == 1 else jnp.concatenate([fc_ref[j] for j in grp], axis=0)
        rows_g = rows_t * len(grp)
        la = jax.nn.log_sigmoid(_dot(fc.astype(BF16), wfg_ref[...]) + bfg_ref[...]) / GLA_TAU
        ri = lax.broadcasted_iota(jnp.int32, (rows_g, rows_g), 0)
        ci = lax.broadcasted_iota(jnp.int32, (rows_g, rows_g), 1)
        tri = jnp.where(((ri >> sh) == (ci >> sh)) & (ci <= ri), 1.0, 0.0).astype(BF16)
        la_hi = la.astype(BF16)
        la_lo = (la - la_hi.astype(F32)).astype(BF16)
        bcum = _dot(tri, la_hi) + _dot(tri, la_lo)
        for gi, j in enumerate(grp):
            for c in range(nchunk):
                rs = slice(c * chunk, (c + 1) * chunk)
                b = bcum[gi * rows_t + c * chunk:gi * rows_t + (c + 1) * chunk]
                bl = b[chunk - 1:chunk]
                qt = (q_ref[j, rs, :] * jnp.exp(b)).astype(BF16)
                kt = (k_ref[j, rs, :] * jnp.exp(-b)).astype(BF16)
                kd = (k_ref[j, rs, :] * jnp.exp(bl - b)).astype(BF16)
                dec = jnp.exp(bl)
                vb = v_ref[j, rs, :].astype(BF16)
                k_blk = jnp.where(diag_k, jnp.concatenate([kt] * H_C, axis=0), 0.0)
                v_blk = jnp.where(diag_v, jnp.concatenate([vb] * H_C, axis=0), 0.0)
                att = jnp.where(causal, _dot_nt(qt, k_blk), 0.0)
                o_intra = _dot(att.astype(BF16), v_blk)
                for h in range(H_C):
                    ks = slice(h * DK_C, (h + 1) * DK_C)
                    vs = slice(h * DV_C, (h + 1) * DV_C)
                    s_t = st[j * H_C + h]
                    o = o_intra[:, vs] + _dot_nt(qt[:, ks], s_t.astype(BF16))
                    st[j * H_C + h] = s_t * dec[:, ks] + _dot_tn(vb[:, vs], kd[:, ks])
                    y_ref[j, rs, vs] = _rms(o, gn) * jax.nn.silu(r_ref[j, rs, vs])

    @pl.when(tb == pl.num_programs(1) - 1)
    def _():
        for j in range(bb):
            for h in range(H_C):
                sfin_ref[j, h] = st[j * H_C + h].T


GLA_BLOCK_CHUNKS = 4
GLA_BATCH_ROWS = 4


def _gla(qc, kc, vc, rc, fc, wfg, bfg, gnorm, layer, s0, batch, seq):
    chunk = math.gcd(seq, GLA_CHUNK)
    nchunk = min(seq // chunk, GLA_BLOCK_CHUNKS)
    rows_t = chunk * nchunk
    nt = seq // rows_t
    bb = math.gcd(batch, max(1, GLA_CHUNK // rows_t) if nt == 1 else GLA_BATCH_ROWS)
    tok = lambda wd: pl.BlockSpec((bb, rows_t, wd), lambda b, t: (b, t, 0))
    st_spec = pl.BlockSpec((bb, H_C, DK_C, DV_C), lambda b, t: (b, 0, 0, 0))
    by_row = lambda a: a.reshape(batch, seq, a.shape[-1])
    y, s_fin = pl.pallas_call(
        functools.partial(_gla_kernel, chunk=chunk, nchunk=nchunk, bb=bb),
        grid=(batch // bb, nt),
        in_specs=[tok(H_C * DK_C), tok(H_C * DK_C), tok(H_C * DV_C), tok(H_C * DV_C), tok(FG_RANK),
                  _resident((FG_RANK, H_C * DK_C), layer), _resident((1, H_C * DK_C), layer),
                  _resident((1, DV_C), layer), st_spec],
        out_specs=[tok(H_C * DV_C), st_spec],
        out_shape=[jax.ShapeDtypeStruct((batch, seq, H_C * DV_C), F32),
                   jax.ShapeDtypeStruct((batch, H_C, DK_C, DV_C), F32)],
        scratch_shapes=[pltpu.VMEM((bb * H_C, DV_C, DK_C), F32)],
        compiler_params=_cparams(("parallel", "arbitrary")),
        name="gla",
    )(by_row(qc), by_row(kc), by_row(vc), by_row(rc), by_row(fc), wfg, bfg, gnorm, s0)
    return y.reshape(batch * seq, H_C * DV_C), s_fin


def _post_kernel(x_ref, oa_ref, gates_ref, yb_ref, yc_ref, woa_ref, woc_ref, wout_ref, gmlp_ref, wup_ref, wdn_ref,
                 gfin_ref, out_ref, *, final):
    y_a = _dot(oa_ref[...].astype(BF16), woa_ref[...])
    y_c = _dot(yc_ref[...].astype(BF16), woc_ref[...])
    gates = gates_ref[...]
    m = (jax.nn.sigmoid(gates[:, :D_MODEL]) * y_a + jax.nn.sigmoid(gates[:, D_MODEL:2 * D_MODEL]) * yb_ref[...]
         + jax.nn.sigmoid(gates[:, 2 * D_MODEL:]) * y_c)
    x1 = x_ref[...] + _dot(m.astype(BF16), wout_ref[...])
    hm = _rms(x1, gmlp_ref[...]).astype(BF16)
    up = _dot(hm, wup_ref[...])
    act = jnp.square(jnp.maximum(up, 0.0)).astype(BF16)
    x2 = x1 + _dot(act, wdn_ref[...])
    out_ref[...] = _rms(x2, gfin_ref[...]) if final else x2


def _post(x, oa, gates, yb, yc, lw, layer, gfin, final, tm):
    m = x.shape[0]
    row = lambda i: (i, 0)
    tok = lambda wd: pl.BlockSpec((tm, wd), row)
    return pl.pallas_call(
        functools.partial(_post_kernel, final=final),
        grid=(m // tm,),
        in_specs=[tok(D_MODEL), tok(D_A), tok(3 * D_MODEL), tok(D_MODEL), tok(H_C * DV_C),
                  _resident((D_A, D_MODEL), layer), _resident((H_C * DV_C, D_MODEL), layer),
                  _resident((D_MODEL, D_MODEL), layer), _resident((1, D_MODEL), layer),
                  _resident((D_MODEL, D_FF), layer), _resident((D_FF, D_MODEL), layer),
                  _resident((1, D_MODEL))],
        out_specs=tok(D_MODEL),
        out_shape=jax.ShapeDtypeStruct((m, D_MODEL), F32),
        compiler_params=_cparams(("parallel",)),
        name="post",
    )(x, oa, gates, yb, yc, lw["woa"], lw["woc"], lw["wout"], lw["gmlp"], lw["wup"], lw["wdn"], gfin)


def _t5_bucket(dist):
    exact = N_BUCKETS // 2
    d = np.maximum(dist, 1).astype(np.float32)
    large = exact + (np.log(d / exact) / np.log(W_MAX / exact) * (N_BUCKETS - exact)).astype(np.int32)
    large = np.minimum(large, N_BUCKETS - 1)
    return np.where(dist < exact, dist, large).astype(np.int32)


def _toeplitz(u, rows, cols, off):
    period = cols + off + 1
    w = jnp.pad(u, ((0, 0), (0, period - u.shape[1])))
    a = jnp.tile(w, (1, rows))[:, :rows * (period - 1)].reshape(u.shape[0], rows, period - 1)
    return a[:, :, off:off + cols]


def _prompt_bias(rel_bias, g, dil):
    x = np.arange(3 * N_BACK - 1)
    bk = _t5_bucket(np.clip(2 * N_BACK - 1 - x, 0, N_BACK) * dil)
    u = rel_bias[bk][:, g * H_A:(g + 1) * H_A].T
    return _toeplitz(u, N_BACK, 2 * N_BACK, N_BACK - 1)


def _sample_bias(rel_bias, wbuf, t_new):
    n = wbuf + t_new
    x = np.arange(n + t_new - 1)
    bk = _t5_bucket(np.clip(n - 1 - x, 0, W_MAX))
    tab = rel_bias[bk]
    per_g = [_toeplitz(tab[:, g * H_A:(g + 1) * H_A].T, t_new, n, t_new - 1) for g in range(G_A)]
    full = jnp.concatenate(per_g, axis=1)
    return full[:, :, :wbuf], full[:, :, wbuf:]


def _weights(w_in, w_o_a, w_glu, w_fg2, b_fg, gla_norm, w_o_c, w_out, norm_mix, norm_mlp, w_up, w_down):
    depth = w_in.shape[0]
    return dict(
        win=jnp.transpose(w_in, (0, 2, 1)).astype(BF16),
        gmix=norm_mix.reshape(depth, 1, D_MODEL),
        wglu=w_glu.astype(BF16),
        wfg=w_fg2.astype(BF16),
        bfg=b_fg.reshape(depth, 1, H_C * DK_C),
        gnorm=gla_norm.reshape(depth, 1, DV_C),
        woa=w_o_a.astype(BF16), woc=w_o_c.astype(BF16), wout=w_out.astype(BF16),
        gmlp=norm_mlp.reshape(depth, 1, D_MODEL),
        wup=w_up.astype(BF16), wdn=w_down.astype(BF16),
    )


def kernel(x_prompt, x_sample, cache_k_win, cache_v_win, state_ssm_re, state_ssm_im, state_gla, rel_bias, norm_mix, w_in, w_o_a, s5_log_dt, s5_a_re, s5_a_im, s5_b_re, s5_b_im, s5_c_re, s5_c_im, s5_d, w_glu, w_fg2, b_fg, gla_norm, w_o_c, w_out, norm_mlp, w_up, w_down, norm_final):
    bp, sp, _ = x_prompt.shape
    bs, ts, _ = x_sample.shape
    wbuf = cache_k_win.shape[2]
    mp, ms = bp * sp, bs * ts
    gfin = norm_final.reshape(1, D_MODEL)
    kt_cache = jnp.transpose(cache_k_win, (0, 1, 3, 4, 2))
    vt_cache = jnp.transpose(cache_v_win, (0, 1, 3, 4, 2))
    pbias = [_prompt_bias(rel_bias, g, dil) for g, (_, dil) in enumerate(PATTERNS)]
    sbias_c, sbias_n = _sample_bias(rel_bias, wbuf, ts)

    xp = x_prompt.reshape(mp, D_MODEL)
    xs = x_sample.reshape(ms, D_MODEL)
    outs = {k: [] for k in ("kp", "vp", "kn", "vn", "rp", "ip", "rs", "is", "gp", "gs")}
    lw = _weights(w_in, w_o_a, w_glu, w_fg2, b_fg, gla_norm, w_o_c, w_out, norm_mix, norm_mlp, w_up, w_down)
    s5p = _s5_params(s5_log_dt, s5_a_re, s5_a_im, s5_b_re, s5_b_im, s5_c_re, s5_c_im, s5_d)
    assert sp <= W_MAX
    for l in range(DEPTH):
        final = l == DEPTH - 1

        qa, ka, va, ub, qc, kc, vc, rc, fc, gates, kt, vt = _in_proj(xp, lw["gmix"], lw["win"], l, TM, (bp, sp))
        oa = _attn_prompt(qa, ka, va, pbias, bp, sp)
        zeros = jnp.zeros((bp, D_STATE), F32)
        yb, xr, xi = _s5(ub.reshape(bp, sp, D_B), zeros, zeros, s5p, lw["wglu"], l, S5_TC)
        yb = yb.reshape(mp, D_MODEL)
        yc, gst = _gla(qc, kc, vc, rc, fc, lw["wfg"], lw["bfg"], lw["gnorm"], l,
                       jnp.zeros((bp, H_C, DK_C, DV_C), F32), bp, sp)
        xp = _post(xp, oa, gates, yb, yc, lw, l, gfin, final, TM)
        outs["kp"].append(kt)
        outs["vp"].append(vt)
        outs["rp"].append(xr.reshape(bp, G_B, N_B))
        outs["ip"].append(xi.reshape(bp, G_B, N_B))
        outs["gp"].append(gst)

        qa, ka, va, ub, qc, kc, vc, rc, fc, gates = _in_proj(xs, lw["gmix"], lw["win"], l, TM)
        q4 =qa.reshape(bs, ts, G_A, H_A, HD_A).transpose(0, 3, 2, 1, 4).reshape(bs, H_A, G_A * ts, HD_A)
        knt = ka.reshape(bs, ts, H_A, HD_A).transpose(0, 2, 3, 1)
        vnt = va.reshape(bs, ts, H_A, HD_A).transpose(0, 2, 3, 1)
        o4 = _attn_sample(q4, kt_cache, vt_cache, l, knt, vnt, sbias_c, sbias_n)
        oa = o4.transpose(0, 2, 1, 3).reshape(ms, D_A)
        yb, xr, xi = _s5(ub.reshape(bs, ts, D_B), state_ssm_re[l].reshape(bs, D_STATE),
                         state_ssm_im[l].reshape(bs, D_STATE), s5p, lw["wglu"], l, ts)
        yb = yb.reshape(ms, D_MODEL)
        yc, gst = _gla(qc, kc, vc, rc, fc, lw["wfg"], lw["bfg"], lw["gnorm"], l, state_gla[l], bs, ts)
        xs = _post(xs, oa, gates, yb, yc, lw, l, gfin, final, TM)
        outs["kn"].append(knt)
        outs["vn"].append(vnt)
        outs["rs"].append(xr.reshape(bs, G_B, N_B))
        outs["is"].append(xi.reshape(bs, G_B, N_B))
        outs["gs"].append(gst)

    st = {k: jnp.stack(v) for k, v in outs.items()}
    tail = ((0, 0), (0, 0), (0, 0), (0, 0), (LANES - ts, 0))
    kt_new, vt_new = _cache_update(kt_cache, vt_cache, jnp.pad(st["kn"], tail), jnp.pad(st["vn"], tail), ts)
    to_rows = lambda t: jnp.transpose(t, (0, 1, 4, 2, 3))
    return (xp.reshape(bp, sp, D_MODEL), xs.reshape(bs, ts, D_MODEL),
            to_rows(st["kp"]), to_rows(st["vp"]), to_rows(kt_new), to_rows(vt_new),
            st["rp"], st["ip"], st["rs"], st["is"], st["gp"], st["gs"])
```

```python
import functools
import math

import numpy as np
import jax
import jax.numpy as jnp
from jax import lax
from jax.experimental import pallas as pl
from jax.experimental.pallas import tpu as pltpu

F32 = jnp.float32
BF16 = jnp.bfloat16

LANES = 128
D_MODEL = 1024
DEPTH = 4
H_A = 8
HD_A = 64
D_A = H_A * HD_A
N_PAIR = D_A // LANES
PATTERNS = ((128, 1), (512, 4), (2048, 16))
G_A = len(PATTERNS)
N_BACK = 128
W_MAX = 2048
N_BUCKETS = 32
P_B = 16
D_B = 512
G_B = D_B // P_B
N_B = 64
D_STATE = G_B * N_B
H_C = 4
DK_C = 64
DV_C = 128
FG_RANK = 16
FG_PAD = 128
GLA_TAU = 16.0
GLA_CHUNK = 64
D_FF = 4 * D_MODEL
EPS = 1e-6
NEG = -1e30
LOG2E = 1.4426950408889634

SEG_WIDTHS = (G_A * D_A, D_A, D_A, D_B, H_C * DK_C, H_C * DK_C, H_C * DV_C, H_C * DV_C, FG_RANK, 3 * D_MODEL)
SEG_OFFS = tuple(int(v) for v in np.cumsum((0,) + SEG_WIDTHS))
IN_COLS = SEG_OFFS[-1]

VMEM_LIMIT = 56 * 1024 * 1024
TM = 256


def _cparams(sem):
    return pltpu.CompilerParams(dimension_semantics=sem, vmem_limit_bytes=VMEM_LIMIT)


def _resident(shape, layer=None):
    nd = len(shape)
    if layer is None:
        return pl.BlockSpec(shape, lambda *_: (0,) * nd, pipeline_mode=pl.Buffered(1))
    return pl.BlockSpec((None,) + tuple(shape), lambda *_: (layer,) + (0,) * nd, pipeline_mode=pl.Buffered(1))


def _rms(x, g):
    return x * lax.rsqrt(jnp.mean(x * x, axis=-1, keepdims=True) + EPS) * g


def _dot(a, b):
    return jnp.dot(a, b, preferred_element_type=F32)


def _dot_nt(a, b):
    return lax.dot_general(a, b, (((1,), (1,)), ((), ())), preferred_element_type=F32)


def _dot_tn(a, b):
    return lax.dot_general(a, b, (((0,), (0,)), ((), ())), preferred_element_type=F32)


def _in_proj_kernel(x_ref, g_ref, w_ref, *out_refs):
    hb = _rms(x_ref[...], g_ref[...]).astype(BF16)
    for s, o_ref in enumerate(out_refs[:len(SEG_WIDTHS)]):
        z = _dot_nt(hb, w_ref[SEG_OFFS[s]:SEG_OFFS[s + 1], :])
        if s in (0, 4):
            z = z * 0.125
        o_ref[...] = z
        if s in (1, 2) and len(out_refs) > len(SEG_WIDTHS):
            out_refs[len(SEG_WIDTHS) + s - 1][0] = z.T.reshape(H_A, HD_A, z.shape[0])


def _in_proj(x, g, w, layer, tm, kv_t_shape=None):
    m = x.shape[0]
    row = lambda i: (i, 0)
    out_specs = [pl.BlockSpec((tm, wd), row) for wd in SEG_WIDTHS]
    out_shape = [jax.ShapeDtypeStruct((m, wd), F32) for wd in SEG_WIDTHS]
    if kv_t_shape is not None:
        batch, seq = kv_t_shape
        per = seq // tm
        out_specs += [pl.BlockSpec((1, H_A, HD_A, tm), lambda i: (i // per, 0, 0, i % per))] * 2
        out_shape += [jax.ShapeDtypeStruct((batch, H_A, HD_A, seq), F32)] * 2
    return pl.pallas_call(
        _in_proj_kernel,
        grid=(m // tm,),
        in_specs=[pl.BlockSpec((tm, D_MODEL), row), _resident((1, D_MODEL), layer),
                  _resident((IN_COLS, D_MODEL), layer)],
        out_specs=out_specs,
        out_shape=out_shape,
        compiler_params=_cparams(("parallel",)),
        name="in_proj",
    )(x, g, w)


ATTN_UNITS = 16


def _attn_prompt_kernel(q0_ref, q1_ref, q2_ref, k_ref, v_ref, b0_ref, b1_ref, b2_ref, o_ref,
                        qs, k1, v1, k4, v4, k16, v16, tq, tk, tv, og, lg, dg, *, seq):
    nu = seq // N_BACK
    c4 = seq // 4
    left = lax.broadcasted_iota(jnp.int32, (N_BACK, LANES), 1) < HD_A

    def put_q(g, u, q):
        q = q * LOG2E
        qs[g, 2 * N_BACK * u:2 * N_BACK * u + N_BACK, :] = jnp.where(left, q, 0.0).astype(BF16)
        qs[g, 2 * N_BACK * u + N_BACK:2 * N_BACK * (u + 1), :] = jnp.where(left, 0.0, q).astype(BF16)

    @pl.when((pl.program_id(0) == 0) & (pl.program_id(1) == 0))
    def _():
        zpad = jnp.zeros((N_BACK, LANES), BF16)
        for buf in (k1, k4, v1, v4):
            buf[0:N_BACK, 0:LANES] = zpad
        for buf in (v1, v4, v16):
            buf[:, LANES:] = jnp.ones((buf.shape[0], LANES), BF16)

    k1[N_BACK:N_BACK + seq, :] = k_ref[...].astype(BF16)
    v1[N_BACK:N_BACK + seq, 0:LANES] = v_ref[...].astype(BF16)
    for r in range(4):
        by4 = pl.ds(r, c4, stride=4)
        tk[r * c4:(r + 1) * c4, :] = k_ref[by4, :]
        tv[r * c4:(r + 1) * c4, :] = v_ref[by4, :]
        tq[r * c4:(r + 1) * c4, :] = q2_ref[by4, :]
    k4[N_BACK:N_BACK + seq, :] = tk[...].astype(BF16)
    v4[N_BACK:N_BACK + seq, 0:LANES] = tv[...].astype(BF16)
    for u in range(nu):
        put_q(0, u, q0_ref[u * N_BACK:(u + 1) * N_BACK, :])
        put_q(1, u, q1_ref[pl.ds(u // 4 + 4 * N_BACK * (u % 4), N_BACK, stride=4), :])
        by16 = pl.ds((u % 4) * c4 + u // 4, N_BACK, stride=4)
        put_q(2, u, tq[by16, :])
        k16[u * N_BACK:(u + 1) * N_BACK, :] = tk[by16, :].astype(BF16)
        v16[u * N_BACK:(u + 1) * N_BACK, 0:LANES] = tv[by16, :].astype(BF16)

    shp = (2 * N_BACK, 2 * N_BACK)
    qi = lax.broadcasted_iota(jnp.int32, shp, 0) & (N_BACK - 1)
    kk = lax.broadcasted_iota(jnp.int32, shp, 1)
    step = N_BACK + qi - kk
    band = (step >= 0) & (step <= N_BACK)
    tri = band[:, N_BACK:]

    def units(it, carry, *, g, kbuf, vbuf, b_ref, class_blocks, staged):
        def scores(u):
            q = qs[g, pl.ds(pl.multiple_of(u * 2 * N_BACK, 2 * N_BACK), 2 * N_BACK), :]
            k0 = pl.multiple_of(u * N_BACK, N_BACK)
            if class_blocks > 1:
                krows = pl.ds(k0, 2 * N_BACK)
                first = (u & (class_blocks - 1)) == 0
                bias = b_ref[0, pl.ds(pl.multiple_of(jnp.where(first, 2 * N_BACK, 0), 2 * N_BACK), 2 * N_BACK), :]
                valid = band
            else:
                krows = pl.ds(k0, N_BACK)
                bias = b_ref[0, 0:2 * N_BACK, N_BACK:]
                valid = tri
            return k0, krows, jnp.where(valid, _dot_nt(q, kbuf[krows, :]) + bias, NEG)

        def weights(s):
            mx = jnp.max(s, axis=-1, keepdims=True)
            return mx, jnp.exp2(s - mx).astype(BF16)

        def finish(k0, krows, mx, p):
            pv = _dot(p, vbuf[krows, :])
            mxb = jnp.broadcast_to(mx, (2 * N_BACK, LANES))
            rows = pl.ds(k0, N_BACK)
            og[g, rows, :] = jnp.where(left, pv[0:N_BACK, 0:LANES], pv[N_BACK:, 0:LANES])
            dg[g, rows, :] = jnp.where(left, pv[0:N_BACK, LANES:], pv[N_BACK:, LANES:])
            lg[g, rows, :] = jnp.where(left, mxb[0:N_BACK], mxb[N_BACK:])

        us = [it * ATTN_UNITS + i for i in range(ATTN_UNITS)]
        if staged:
            sc = [scores(u) for u in us]
            wt = [weights(s) for _, _, s in sc]
            for (k0, krows, _), (mx, p) in zip(sc, wt):
                finish(k0, krows, mx, p)
        else:
            for u in us:
                k0, krows, s = scores(u)
                finish(k0, krows, *weights(s))
        return carry

    for g, (kbuf, vbuf, b_ref, class_blocks) in enumerate(((k1, v1, b0_ref, nu), (k4, v4, b1_ref, nu // 4),
                                                           (k16, v16, b2_ref, 1))):
        lax.fori_loop(0, nu // ATTN_UNITS, functools.partial(units, g=g, kbuf=kbuf, vbuf=vbuf, b_ref=b_ref,
                                                             class_blocks=class_blocks,
                                                             staged=class_blocks == 1), 0)

    for u in range(nu):
        by16 = pl.ds((u % 4) * c4 + u // 4, N_BACK, stride=4)
        tq[by16, :] = og[2, u * N_BACK:(u + 1) * N_BACK, :]
        tk[by16, :] = lg[2, u * N_BACK:(u + 1) * N_BACK, :]
        tv[by16, :] = dg[2, u * N_BACK:(u + 1) * N_BACK, :]
    for r in range(4):
        for jb in range(c4 // N_BACK):
            rows4 = slice(r * c4 + jb * N_BACK, r * c4 + (jb + 1) * N_BACK)
            tok = pl.ds(r + 4 * N_BACK * jb, N_BACK, stride=4)
            ms = [lg[0, tok, :], lg[1, rows4, :], tk[rows4, :]]
            mx = jnp.maximum(jnp.maximum(ms[0], ms[1]), ms[2])
            ws = [jnp.exp2(m - mx) for m in ms]
            num = ws[0] * og[0, tok, :] + ws[1] * og[1, rows4, :] + ws[2] * tq[rows4, :]
            den = ws[0] * dg[0, tok, :] + ws[1] * dg[1, rows4, :] + ws[2] * tv[rows4, :]
            o_ref[tok, :] = num / den


def _attn_prompt(qa, ka, va, biasmats, batch, seq):
    assert tuple(d for _, d in PATTERNS) == (1, 4, 16) and seq == 16 * N_BACK
    blk = (seq, LANES)
    qspec = lambda g: pl.BlockSpec(blk, lambda b, p: (b, g * N_PAIR + p))
    kspec = pl.BlockSpec(blk, lambda b, p: (b, p))
    bspec = pl.BlockSpec((1, 4 * N_BACK, 2 * N_BACK), lambda b, p: (p, 0, 0))
    before = np.arange(2 * N_BACK) < N_BACK
    pair_bias = []
    for bm in biasmats:
        t = (bm * LOG2E).reshape(N_PAIR, 2 * N_BACK, 2 * N_BACK)
        pair_bias.append(jnp.concatenate([t, jnp.where(before, NEG, t)], axis=1))
    act = lambda rows: pltpu.VMEM((rows, LANES), BF16)
    val = lambda rows: pltpu.VMEM((rows, 2 * LANES), BF16)
    return pl.pallas_call(
        functools.partial(_attn_prompt_kernel, seq=seq),
        grid=(batch, N_PAIR),
        in_specs=[qspec(0), qspec(1), qspec(2), kspec, kspec, bspec, bspec, bspec],
        out_specs=kspec,
        out_shape=jax.ShapeDtypeStruct((batch * seq, D_A), F32),
        scratch_shapes=[pltpu.VMEM((G_A, 2 * seq, LANES), BF16),
                        act(N_BACK + seq), val(N_BACK + seq), act(N_BACK + seq), val(N_BACK + seq),
                        act(seq), val(seq),
                        pltpu.VMEM((seq, LANES), F32), pltpu.VMEM((seq, LANES), F32), pltpu.VMEM((seq, LANES), F32),
                        pltpu.VMEM((G_A, seq, LANES), F32), pltpu.VMEM((G_A, seq, LANES), F32),
                        pltpu.VMEM((G_A, seq, LANES), F32)],
        compiler_params=_cparams(("arbitrary", "arbitrary")),
        name="attn_prompt",
    )(qa, qa, qa, ka, va, *pair_bias)


def _attn_sample_kernel(q_ref, kt_ref, vt_ref, knt_ref, vnt_ref, bc_ref, bn_ref, o_ref, *, wbuf, t_new):
    rows = G_A * t_new

    def allowed(shape, key0):
        rr = lax.broadcasted_iota(jnp.int32, shape, 0)
        col = lax.broadcasted_iota(jnp.int32, shape, 1)
        grp = rr >> (t_new.bit_length() - 1)
        dist = wbuf + (rr - grp * t_new) - (col + key0)
        dm1 = jnp.where(grp == 0, PATTERNS[0][1] - 1, jnp.where(grp == 1, PATTERNS[1][1] - 1, PATTERNS[2][1] - 1))
        win = jnp.where(grp == 0, PATTERNS[0][0], jnp.where(grp == 1, PATTERNS[1][0], PATTERNS[2][0]))
        return (dist >= 0) & ((dist & dm1) == 0) & (dist <= win)

    ok_c = allowed((rows, wbuf), 0)
    ok_n = allowed((rows, t_new), wbuf)
    for h in range(H_A):
        qh = q_ref[0, h].astype(BF16)
        sc = jnp.where(ok_c, _dot(qh, kt_ref[0, 0, h].astype(BF16)) + bc_ref[h], NEG)
        sn = jnp.where(ok_n, _dot(qh, knt_ref[0, h].astype(BF16)) + bn_ref[h], NEG)
        mx = jnp.maximum(jnp.max(sc, axis=-1, keepdims=True), jnp.max(sn, axis=-1, keepdims=True))
        mt = jnp.maximum(jnp.maximum(mx[0:t_new], mx[t_new:2 * t_new]), mx[2 * t_new:3 * t_new])
        mall = jnp.concatenate([mt, mt, mt], axis=0)
        pc = jnp.exp(sc - mall)
        pn = jnp.exp(sn - mall)
        den = jnp.sum(pc, axis=-1, keepdims=True) + jnp.sum(pn, axis=-1, keepdims=True)
        acc = (_dot_nt(pc.astype(BF16), vt_ref[0, 0, h].astype(BF16))
               + _dot_nt(pn.astype(BF16), vnt_ref[0, h].astype(BF16)))
        dt = den[0:t_new] + den[t_new:2 * t_new] + den[2 * t_new:3 * t_new]
        at = acc[0:t_new] + acc[t_new:2 * t_new] + acc[2 * t_new:3 * t_new]
        o_ref[0, h] = at / dt


def _attn_sample(q, kt_cache, vt_cache, layer, knt, vnt, bias_c, bias_n):
    batch, wbuf = kt_cache.shape[1], kt_cache.shape[4]
    t_new = knt.shape[3]
    rows = G_A * t_new
    b4 = lambda b: (b, 0, 0, 0)
    cache_spec = pl.BlockSpec((1, 1, H_A, HD_A, wbuf), lambda b: (layer, b, 0, 0, 0))
    new_spec = pl.BlockSpec((1, H_A, HD_A, t_new), b4)
    return pl.pallas_call(
        functools.partial(_attn_sample_kernel, wbuf=wbuf, t_new=t_new),
        grid=(batch,),
        in_specs=[pl.BlockSpec((1, H_A, rows, HD_A), b4), cache_spec, cache_spec, new_spec, new_spec,
                  _resident((H_A, rows, wbuf)), _resident((H_A, rows, t_new))],
        out_specs=pl.BlockSpec((1, H_A, t_new, HD_A), b4),
        out_shape=jax.ShapeDtypeStruct((batch, H_A, t_new, HD_A), F32),
        compiler_params=_cparams(("parallel",)),
        name="attn_sample",
    )(q, kt_cache, vt_cache, knt, vnt, bias_c, bias_n)


def _cache_update_kernel(kt_ref, vt_ref, kn_ref, vn_ref, ko_ref, vo_ref, *, wbuf, t_new):
    is_new = lax.broadcasted_iota(jnp.int32, (HD_A, LANES), 1) >= LANES - t_new
    for src, new, dst in ((kt_ref, kn_ref, ko_ref), (vt_ref, vn_ref, vo_ref)):
        for h in range(H_A):
            moved = pltpu.roll(src[0, 0, h], wbuf - t_new, 1)
            dst[0, 0, h, :, 0:wbuf - LANES] = moved[:, 0:wbuf - LANES]
            dst[0, 0, h, :, wbuf - LANES:wbuf] = jnp.where(is_new, new[0, 0, h], moved[:, wbuf - LANES:wbuf])


def _cache_update(kt_cache, vt_cache, kn_tail, vn_tail, t_new):
    depth, batch, _, _, wbuf = kt_cache.shape
    idx = lambda l, b: (l, b, 0, 0, 0)
    cache_spec = pl.BlockSpec((1, 1, H_A, HD_A, wbuf), idx)
    tail_spec = pl.BlockSpec((1, 1, H_A, HD_A, LANES), idx)
    shape = jax.ShapeDtypeStruct(kt_cache.shape, F32)
    return pl.pallas_call(
        functools.partial(_cache_update_kernel, wbuf=wbuf, t_new=t_new),
        grid=(depth, batch),
        in_specs=[cache_spec, cache_spec, tail_spec, tail_spec],
        out_specs=[cache_spec, cache_spec],
        out_shape=[shape, shape],
        compiler_params=_cparams(("parallel", "parallel")),
        name="cache_update",
    )(kt_cache, vt_cache, kn_tail, vn_tail)


S5_KB = 128
S5_NBLK = D_B // S5_KB
S5_SB = D_STATE // S5_NBLK
S5_TC = 128


S5_SCAN_UNROLL = 8


def _s5_kernel(u_ref, x0r_ref, x0i_ref, ar_ref, ai_ref, bre_ref, bim_ref, cre_ref, cim_ref, d_ref, wglu_ref,
               yb_ref, xr_out, xi_out, ut, ybt, bur, bui, sr, si, *, nb, tc):
    @pl.when(pl.program_id(0) == 0)
    def _():
        sr[...] = x0r_ref[...]
        si[...] = x0i_ref[...]

    for b in range(nb):
        for c in range(D_B // LANES):
            ut[c, pl.ds(b, tc, stride=nb), :] = u_ref[b, :, c * LANES:(c + 1) * LANES]
    u = jnp.concatenate([ut[c] for c in range(D_B // LANES)], axis=1)
    ub = u.astype(BF16)
    for j in range(S5_NBLK):
        uj = ub[:, j * S5_KB:(j + 1) * S5_KB]
        bur[:, j * S5_SB:(j + 1) * S5_SB] = _dot(uj, bre_ref[j])
        bui[:, j * S5_SB:(j + 1) * S5_SB] = _dot(uj, bim_ref[j])

    ar = ar_ref[...]
    ai = ai_ref[...]

    def step(t, carry):
        xr, xi = carry
        rows = pl.ds(pl.multiple_of(t * nb, nb), nb)
        nr = ar * xr - ai * xi + bur[rows, :]
        ni = ar * xi + ai * xr + bui[rows, :]
        bur[rows, :] = nr
        bui[rows, :] = ni
        return nr, ni

    xr, xi = lax.fori_loop(0, tc, step, (sr[...], si[...]), unroll=S5_SCAN_UNROLL)
    sr[...] = xr
    si[...] = xi
    xr_out[...] = xr
    xi_out[...] = xi

    ys = []
    for j in range(S5_NBLK):
        cs = slice(j * S5_SB, (j + 1) * S5_SB)
        ys.append(_dot(bur[:, cs].astype(BF16), cre_ref[j]) - _dot(bui[:, cs].astype(BF16), cim_ref[j]))
    y = jnp.concatenate(ys, axis=-1) + d_ref[...] * u
    ab = _dot(jax.nn.gelu(y).astype(BF16), wglu_ref[...])
    for c in range(D_MODEL // LANES):
        cs = slice(c * LANES, (c + 1) * LANES)
        ybt[c] = ab[:, cs] * jax.nn.sigmoid(ab[:, D_MODEL + c * LANES:D_MODEL + (c + 1) * LANES])
    for b in range(nb):
        for c in range(D_MODEL // LANES):
            yb_ref[b, :, c * LANES:(c + 1) * LANES] = ybt[c, pl.ds(b, tc, stride=nb), :]


def _s5(u, x0r, x0i, prm, wglu, layer, tc):
    nb, seq, _ = u.shape
    r = tc * nb
    blk = lambda i: (0, i, 0)
    st = pl.BlockSpec((nb, D_STATE), lambda i: (0, 0))
    return pl.pallas_call(
        functools.partial(_s5_kernel, nb=nb, tc=tc),
        grid=(seq // tc,),
        in_specs=[pl.BlockSpec((nb, tc, D_B), blk), st, st,
                  _resident((1, D_STATE), layer), _resident((1, D_STATE), layer),
                  _resident((S5_NBLK, S5_KB, S5_SB), layer), _resident((S5_NBLK, S5_KB, S5_SB), layer),
                  _resident((S5_NBLK, S5_SB, S5_KB), layer), _resident((S5_NBLK, S5_SB, S5_KB), layer),
                  _resident((1, D_B), layer), _resident((D_B, 2 * D_MODEL), layer)],
        out_specs=[pl.BlockSpec((nb, tc, D_MODEL), blk), st, st],
        out_shape=[jax.ShapeDtypeStruct((nb, seq, D_MODEL), F32),
                   jax.ShapeDtypeStruct((nb, D_STATE), F32), jax.ShapeDtypeStruct((nb, D_STATE), F32)],
        scratch_shapes=[pltpu.VMEM((D_B // LANES, r, LANES), F32), pltpu.VMEM((D_MODEL // LANES, r, LANES), F32),
                        pltpu.VMEM((r, D_STATE), F32), pltpu.VMEM((r, D_STATE), F32),
                        pltpu.VMEM((nb, D_STATE), F32), pltpu.VMEM((nb, D_STATE), F32)],
        compiler_params=_cparams(("arbitrary",)),
        name="s5",
    )(u, x0r, x0i, prm["ar"], prm["ai"], prm["bre"], prm["bim"], prm["cre"], prm["cim"], prm["d"], wglu)


def _s5_params(log_dt, a_re, a_im, b_re, b_im, c_re, c_im, d_skip):
    depth = log_dt.shape[0]
    dt = jnp.exp(log_dt)[..., None]
    mag = jnp.exp(a_re * dt)
    ab_re = mag * jnp.cos(a_im * dt)
    ab_im = mag * jnp.sin(a_im * dt)
    den = a_re * a_re + a_im * a_im
    nr = ab_re - 1.0
    co_re = (nr * a_re + ab_im * a_im) / den
    co_im = (ab_im * a_re - nr * a_im) / den
    bb_re = co_re[..., None] * b_re - co_im[..., None] * b_im
    bb_im = co_re[..., None] * b_im + co_im[..., None] * b_re
    gpb = G_B // S5_NBLK
    eye = jnp.eye(gpb, dtype=F32)

    def pack_b(bb):
        t = bb.reshape(depth, S5_NBLK, gpb, N_B, P_B)
        m = jnp.einsum('ljgnp,gh->ljgphn', t, eye)
        return m.reshape(depth, S5_NBLK, gpb * P_B, gpb * N_B).astype(BF16)

    def pack_c(cc):
        t = cc.reshape(depth, S5_NBLK, gpb, P_B, N_B)
        m = jnp.einsum('ljgpn,gh->ljgnhp', t, eye)
        return m.reshape(depth, S5_NBLK, gpb * N_B, gpb * P_B).astype(BF16)

    return dict(ar=ab_re.reshape(depth, 1, D_STATE), ai=ab_im.reshape(depth, 1, D_STATE),
                bre=pack_b(bb_re), bim=pack_b(bb_im), cre=pack_c(c_re), cim=pack_c(c_im),
                d=d_skip.reshape(depth, 1, D_B))


def _gla_kernel(q_ref, k_ref, v_ref, r_ref, fc_ref, wfg_ref, bfg_ref, gn_ref, s0_ref, y_ref, sfin_ref, st,
                *, chunk, nchunk, bb):
    tb = pl.program_id(1)
    dk_all, dv_all = H_C * DK_C, H_C * DV_C

    def head_of(shape, dim, width):
        return lax.broadcasted_iota(jnp.int32, shape, dim) >> (width.bit_length() - 1)

    tok_all = H_C * chunk
    diag_k = head_of((tok_all, dk_all), 0, chunk) == head_of((tok_all, dk_all), 1, DK_C)
    diag_v = head_of((tok_all, dv_all), 0, chunk) == head_of((tok_all, dv_all), 1, DV_C)

    @pl.when(tb == 0)
    def _():
        for j in range(bb):
            for h in range(H_C):
                st[j * H_C + h] = s0_ref[j, h].T

    rows_t = chunk * nchunk
    sh = chunk.bit_length() - 1
    col_tok = lax.broadcasted_iota(jnp.int32, (chunk, tok_all), 1) & (chunk - 1)
    causal = col_tok <= lax.broadcasted_iota(jnp.int32, (chunk, tok_all), 0)
    gn = gn_ref[...]
    groups = [list(range(bb))] if rows_t < GLA_CHUNK else [[j] for j in range(bb)]
    for grp in groups:
        fc = fc_ref[grp[0]] if len(grp) == 1 else jnp.concatenate([fc_ref[j] for j in grp], axis=0)
        rows_g = rows_t * len(grp)
        la = jax.nn.log_sigmoid(_dot(fc.astype(BF16), wfg_ref[...]) + bfg_ref[...]) / GLA_TAU
        ri = lax.broadcasted_iota(jnp.int32, (rows_g, rows_g), 0)
        ci = lax.broadcasted_iota(jnp.int32, (rows_g, rows_g), 1)
        tri = jnp.where(((ri >> sh) == (ci >> sh)) & (ci <= ri), 1.0, 0.0).astype(BF16)
        la_hi = la.astype(BF16)
        la_lo = (la - la_hi.astype(F32)).astype(BF16)
        bcum = _dot(tri, la_hi) + _dot(tri, la_lo)
        for gi, j in enumerate(grp):
            for c in range(nchunk):
                rs = slice(c * chunk, (c + 1) * chunk)
                b = bcum[gi * rows_t + c * chunk:gi * rows_t + (c + 1) * chunk]
                bl = b[chunk - 1:chunk]
                qt = (q_ref[j, rs, :] * jnp.exp(b)).astype(BF16)
                kt = (k_ref[j, rs, :] * jnp.exp(-b)).astype(BF16)
                kd = (k_ref[j, rs, :] * jnp.exp(bl - b)).astype(BF16)
                dec = jnp.exp(bl)
                vb = v_ref[j, rs, :].astype(BF16)
                k_blk = jnp.where(diag_k, jnp.concatenate([kt] * H_C, axis=0), 0.0)
                v_blk = jnp.where(diag_v, jnp.concatenate([vb] * H_C, axis=0), 0.0)
                att = jnp.where(causal, _dot_nt(qt, k_blk), 0.0)
                o_intra = _dot(att.astype(BF16), v_blk)
                for h in range(H_C):
                    ks = slice(h * DK_C, (h + 1) * DK_C)
                    vs = slice(h * DV_C, (h + 1) * DV_C)
                    s_t = st[j * H_C + h]
                    o = o_intra[:, vs] + _dot_nt(qt[:, ks], s_t.astype(BF16))
                    st[j * H_C + h] = s_t * dec[:, ks] + _dot_tn(vb[:, vs], kd[:, ks])
                    y_ref[j, rs, vs] = _rms(o, gn) * jax.nn.silu(r_ref[j, rs, vs])

    @pl.when(tb == pl.num_programs(1) - 1)
    def _():
        for j in range(bb):
            for h in range(H_C):
                sfin_ref[j, h] = st[j * H_C + h].T


GLA_BLOCK_CHUNKS = 4
GLA_BATCH_ROWS = 4


def _gla(qc, kc, vc, rc, fc, wfg, bfg, gnorm, layer, s0, batch, seq):
    chunk = math.gcd(seq, GLA_CHUNK)
    nchunk = min(seq // chunk, GLA_BLOCK_CHUNKS)
    rows_t = chunk * nchunk
    nt = seq // rows_t
    bb = math.gcd(batch, max(1, GLA_CHUNK // rows_t) if nt == 1 else GLA_BATCH_ROWS)
    tok = lambda wd: pl.BlockSpec((bb, rows_t, wd), lambda b, t: (b, t, 0))
    st_spec = pl.BlockSpec((bb, H_C, DK_C, DV_C), lambda b, t: (b, 0, 0, 0))
    by_row = lambda a: a.reshape(batch, seq, a.shape[-1])
    y, s_fin = pl.pallas_call(
        functools.partial(_gla_kernel, chunk=chunk, nchunk=nchunk, bb=bb),
        grid=(batch // bb, nt),
        in_specs=[tok(H_C * DK_C), tok(H_C * DK_C), tok(H_C * DV_C), tok(H_C * DV_C), tok(FG_RANK),
                  _resident((FG_RANK, H_C * DK_C), layer), _resident((1, H_C * DK_C), layer),
                  _resident((1, DV_C), layer), st_spec],
        out_specs=[tok(H_C * DV_C), st_spec],
        out_shape=[jax.ShapeDtypeStruct((batch, seq, H_C * DV_C), F32),
                   jax.ShapeDtypeStruct((batch, H_C, DK_C, DV_C), F32)],
        scratch_shapes=[pltpu.VMEM((bb * H_C, DV_C, DK_C), F32)],
        compiler_params=_cparams(("parallel", "arbitrary")),
        name="gla",
    )(by_row(qc), by_row(kc), by_row(vc), by_row(rc), by_row(fc), wfg, bfg, gnorm, s0)
    return y.reshape(batch * seq, H_C * DV_C), s_fin


def _post_kernel(x_ref, oa_ref, gates_ref, yb_ref, yc_ref, woa_ref, woc_ref, wout_ref, gmlp_ref, wup_ref, wdn_ref,
                 gfin_ref, out_ref, *, final):
    y_a = _dot(oa_ref[...].astype(BF16), woa_ref[...])
    y_c = _dot(yc_ref[...].astype(BF16), woc_ref[...])
    gates = gates_ref[...]
    m = (jax.nn.sigmoid(gates[:, :D_MODEL]) * y_a + jax.nn.sigmoid(gates[:, D_MODEL:2 * D_MODEL]) * yb_ref[...]
         + jax.nn.sigmoid(gates[:, 2 * D_MODEL:]) * y_c)
    x1 = x_ref[...] + _dot(m.astype(BF16), wout_ref[...])
    hm = _rms(x1, gmlp_ref[...]).astype(BF16)
    up = _dot(hm, wup_ref[...])
    act = jnp.square(jnp.maximum(up, 0.0)).astype(BF16)
    x2 = x1 + _dot(act, wdn_ref[...])
    out_ref[...] = _rms(x2, gfin_ref[...]) if final else x2


def _post(x, oa, gates, yb, yc, lw, layer, gfin, final, tm):
    m = x.shape[0]
    row = lambda i: (i, 0)
    tok = lambda wd: pl.BlockSpec((tm, wd), row)
    return pl.pallas_call(
        functools.partial(_post_kernel, final=final),
        grid=(m // tm,),
        in_specs=[tok(D_MODEL), tok(D_A), tok(3 * D_MODEL), tok(D_MODEL), tok(H_C * DV_C),
                  _resident((D_A, D_MODEL), layer), _resident((H_C * DV_C, D_MODEL), layer),
                  _resident((D_MODEL, D_MODEL), layer), _resident((1, D_MODEL), layer),
                  _resident((D_MODEL, D_FF), layer), _resident((D_FF, D_MODEL), layer),
                  _resident((1, D_MODEL))],
        out_specs=tok(D_MODEL),
        out_shape=jax.ShapeDtypeStruct((m, D_MODEL), F32),
        compiler_params=_cparams(("parallel",)),
        name="post",
    )(x, oa, gates, yb, yc, lw["woa"], lw["woc"], lw["wout"], lw["gmlp"], lw["wup"], lw["wdn"], gfin)


def _t5_bucket(dist):
    exact = N_BUCKETS // 2
    d = np.maximum(dist, 1).astype(np.float32)
    large = exact + (np.log(d / exact) / np.log(W_MAX / exact) * (N_BUCKETS - exact)).astype(np.int32)
    large = np.minimum(large, N_BUCKETS - 1)
    return np.where(dist < exact, dist, large).astype(np.int32)


def _toeplitz(u, rows, cols, off):
    period = cols + off + 1
    w = jnp.pad(u, ((0, 0), (0, period - u.shape[1])))
    a = jnp.tile(w, (1, rows))[:, :rows * (period - 1)].reshape(u.shape[0], rows, period - 1)
    return a[:, :, off:off + cols]


def _prompt_bias(rel_bias, g, dil):
    x = np.arange(3 * N_BACK - 1)
    bk = _t5_bucket(np.clip(2 * N_BACK - 1 - x, 0, N_BACK) * dil)
    u = rel_bias[bk][:, g * H_A:(g + 1) * H_A].T
    return _toeplitz(u, N_BACK, 2 * N_BACK, N_BACK - 1)


def _sample_bias(rel_bias, wbuf, t_new):
    n = wbuf + t_new
    x = np.arange(n + t_new - 1)
    bk = _t5_bucket(np.clip(n - 1 - x, 0, W_MAX))
    tab = rel_bias[bk]
    per_g = [_toeplitz(tab[:, g * H_A:(g + 1) * H_A].T, t_new, n, t_new - 1) for g in range(G_A)]
    full = jnp.concatenate(per_g, axis=1)
    return full[:, :, :wbuf], full[:, :, wbuf:]


def _weights(w_in, w_o_a, w_glu, w_fg2, b_fg, gla_norm, w_o_c, w_out, norm_mix, norm_mlp, w_up, w_down):
    depth = w_in.shape[0]
    return dict(
        win=jnp.transpose(w_in, (0, 2, 1)).astype(BF16),
        gmix=norm_mix.reshape(depth, 1, D_MODEL),
        wglu=w_glu.astype(BF16),
        wfg=w_fg2.astype(BF16),
        bfg=b_fg.reshape(depth, 1, H_C * DK_C),
        gnorm=gla_norm.reshape(depth, 1, DV_C),
        woa=w_o_a.astype(BF16), woc=w_o_c.astype(BF16), wout=w_out.astype(BF16),
        gmlp=norm_mlp.reshape(depth, 1, D_MODEL),
        wup=w_up.astype(BF16), wdn=w_down.astype(BF16),
    )


def kernel(x_prompt, x_sample, cache_k_win, cache_v_win, state_ssm_re, state_ssm_im, state_gla, rel_bias, norm_mix, w_in, w_o_a, s5_log_dt, s5_a_re, s5_a_im, s5_b_re, s5_b_im, s5_c_re, s5_c_im, s5_d, w_glu, w_fg2, b_fg, gla_norm, w_o_c, w_out, norm_mlp, w_up, w_down, norm_final):
    bp, sp, _ = x_prompt.shape
    bs, ts, _ = x_sample.shape
    wbuf = cache_k_win.shape[2]
    mp, ms = bp * sp, bs * ts
    gfin = norm_final.reshape(1, D_MODEL)
    kt_cache = jnp.transpose(cache_k_win, (0, 1, 3, 4, 2))
    vt_cache = jnp.transpose(cache_v_win, (0, 1, 3, 4, 2))
    pbias = [_prompt_bias(rel_bias, g, dil) for g, (_, dil) in enumerate(PATTERNS)]
    sbias_c, sbias_n = _sample_bias(rel_bias, wbuf, ts)

    xp = x_prompt.reshape(mp, D_MODEL)
    xs = x_sample.reshape(ms, D_MODEL)
    outs = {k: [] for k in ("kp", "vp", "kn", "vn", "rp", "ip", "rs", "is", "gp", "gs")}
    lw = _weights(w_in, w_o_a, w_glu, w_fg2, b_fg, gla_norm, w_o_c, w_out, norm_mix, norm_mlp, w_up, w_down)
    s5p = _s5_params(s5_log_dt, s5_a_re, s5_a_im, s5_b_re, s5_b_im, s5_c_re, s5_c_im, s5_d)
    assert sp <= W_MAX
    for l in range(DEPTH):
        final = l == DEPTH - 1

        qa, ka, va, ub, qc, kc, vc, rc, fc, gates, kt, vt = _in_proj(xp, lw["gmix"], lw["win"], l, TM, (bp, sp))
        oa = _attn_prompt(qa, ka, va, pbias, bp, sp)
        zeros = jnp.zeros((bp, D_STATE), F32)
        yb, xr, xi = _s5(ub.reshape(bp, sp, D_B), zeros, zeros, s5p, lw["wglu"], l, S5_TC)
        yb = yb.reshape(mp, D_MODEL)
        yc, gst = _gla(qc, kc, vc, rc, fc, lw["wfg"], lw["bfg"], lw["gnorm"], l,
                       jnp.zeros((bp, H_C, DK_C, DV_C), F32), bp, sp)
        xp = _post(xp, oa, gates, yb, yc, lw, l, gfin, final, TM)
        outs["kp"].append(kt)
        outs["vp"].append(vt)
        outs["rp"].append(xr.reshape(bp, G_B, N_B))
        outs["ip"].append(xi.reshape(bp, G_B, N_B))
        outs["gp"].append(gst)

        qa, ka, va, ub, qc, kc, vc, rc, fc, gates = _in_proj(xs, lw["gmix"], lw["win"], l, TM)
        q4 =qa.reshape(bs, ts, G_A, H_A, HD_A).transpose(0, 3, 2, 1, 4).reshape(bs, H_A, G_A * ts, HD_A)
        knt = ka.reshape(bs, ts, H_A, HD_A).transpose(0, 2, 3, 1)
        vnt = va.reshape(bs, ts, H_A, HD_A).transpose(0, 2, 3, 1)
        o4 = _attn_sample(q4, kt_cache, vt_cache, l, knt, vnt, sbias_c, sbias_n)
        oa = o4.transpose(0, 2, 1, 3).reshape(ms, D_A)
        yb, xr, xi = _s5(ub.reshape(bs, ts, D_B), state_ssm_re[l].reshape(bs, D_STATE),
                         state_ssm_im[l].reshape(bs, D_STATE), s5p, lw["wglu"], l, ts)
        yb = yb.reshape(ms, D_MODEL)
        yc, gst = _gla(qc, kc, vc, rc, fc, lw["wfg"], lw["bfg"], lw["gnorm"], l, state_gla[l], bs, ts)
        xs = _post(xs, oa, gates, yb, yc, lw, l, gfin, final, TM)
        outs["kn"].append(knt)
        outs["vn"].append(vnt)
        outs["rs"].append(xr.reshape(bs, G_B, N_B))
        outs["is"].append(xi.reshape(bs, G_B, N_B))
        outs["gs"].append(gst)

    st = {k: jnp.stack(v) for k, v in outs.items()}
    tail = ((0, 0), (0, 0), (0, 0), (0, 0), (LANES - ts, 0))
    kt_new, vt_new = _cache_update(kt_cache, vt_cache, jnp.pad(st["kn"], tail), jnp.pad(st["vn"], tail), ts)
    to_rows = lambda t: jnp.transpose(t, (0, 1, 4, 2, 3))
    return (xp.reshape(bp, sp, D_MODEL), xs.reshape(bs, ts, D_MODEL),
            to_rows(st["kp"]), to_rows(st["vp"]), to_rows(kt_new), to_rows(vt_new),
            st["rp"], st["ip"], st["rs"], st["is"], st["gp"], st["gs"])
```

```python
import functools
import math

import numpy as np
import jax
import jax.numpy as jnp
from jax import lax
from jax.experimental import pallas as pl
from jax.experimental.pallas import tpu as pltpu

F32 = jnp.float32
BF16 = jnp.bfloat16

LANES = 128
D_MODEL = 1024
DEPTH = 4
H_A = 8
HD_A = 64
D_A = H_A * HD_A
N_PAIR = D_A // LANES
PATTERNS = ((128, 1), (512, 4), (2048, 16))
G_A = len(PATTERNS)
N_BACK = 128
W_MAX = 2048
N_BUCKETS = 32
P_B = 16
D_B = 512
G_B = D_B // P_B
N_B = 64
D_STATE = G_B * N_B
H_C = 4
DK_C = 64
DV_C = 128
FG_RANK = 16
FG_PAD = 128
GLA_TAU = 16.0
GLA_CHUNK = 64
D_FF = 4 * D_MODEL
EPS = 1e-6
NEG = -1e30
LOG2E = 1.4426950408889634

SEG_WIDTHS = (G_A * D_A, D_A, D_A, D_B, H_C * DK_C, H_C * DK_C, H_C * DV_C, H_C * DV_C, FG_RANK, 3 * D_MODEL)
SEG_OFFS = tuple(int(v) for v in np.cumsum((0,) + SEG_WIDTHS))
IN_COLS = SEG_OFFS[-1]

VMEM_LIMIT = 56 * 1024 * 1024
TM = 256


def _cparams(sem):
    return pltpu.CompilerParams(dimension_semantics=sem, vmem_limit_bytes=VMEM_LIMIT)


def _resident(shape, layer=None):
    nd = len(shape)
    if layer is None:
        return pl.BlockSpec(shape, lambda *_: (0,) * nd, pipeline_mode=pl.Buffered(1))
    return pl.BlockSpec((None,) + tuple(shape), lambda *_: (layer,) + (0,) * nd, pipeline_mode=pl.Buffered(1))


def _rms(x, g):
    return x * lax.rsqrt(jnp.mean(x * x, axis=-1, keepdims=True) + EPS) * g


def _dot(a, b):
    return jnp.dot(a, b, preferred_element_type=F32)


def _dot_nt(a, b):
    return lax.dot_general(a, b, (((1,), (1,)), ((), ())), preferred_element_type=F32)


def _dot_tn(a, b):
    return lax.dot_general(a, b, (((0,), (0,)), ((), ())), preferred_element_type=F32)


def _in_proj_kernel(x_ref, g_ref, w_ref, *out_refs):
    hb = _rms(x_ref[...], g_ref[...]).astype(BF16)
    for s, o_ref in enumerate(out_refs[:len(SEG_WIDTHS)]):
        z = _dot_nt(hb, w_ref[SEG_OFFS[s]:SEG_OFFS[s + 1], :])
        if s in (0, 4):
            z = z * 0.125
        o_ref[...] = z
        if s in (1, 2) and len(out_refs) > len(SEG_WIDTHS):
            out_refs[len(SEG_WIDTHS) + s - 1][0] = z.T.reshape(H_A, HD_A, z.shape[0])


def _in_proj(x, g, w, layer, tm, kv_t_shape=None):
    m = x.shape[0]
    row = lambda i: (i, 0)
    out_specs = [pl.BlockSpec((tm, wd), row) for wd in SEG_WIDTHS]
    out_shape = [jax.ShapeDtypeStruct((m, wd), F32) for wd in SEG_WIDTHS]
    if kv_t_shape is not None:
        batch, seq = kv_t_shape
        per = seq // tm
        out_specs += [pl.BlockSpec((1, H_A, HD_A, tm), lambda i: (i // per, 0, 0, i % per))] * 2
        out_shape += [jax.ShapeDtypeStruct((batch, H_A, HD_A, seq), F32)] * 2
    return pl.pallas_call(
        _in_proj_kernel,
        grid=(m // tm,),
        in_specs=[pl.BlockSpec((tm, D_MODEL), row), _resident((1, D_MODEL), layer),
                  _resident((IN_COLS, D_MODEL), layer)],
        out_specs=out_specs,
        out_shape=out_shape,
        compiler_params=_cparams(("parallel",)),
        name="in_proj",
    )(x, g, w)


ATTN_UNITS = 16


def _attn_prompt_kernel(q0_ref, q1_ref, q2_ref, k_ref, v_ref, b0_ref, b1_ref, b2_ref, o_ref,
                        qs, k1, v1, k4, v4, k16, v16, tq, tk, tv, og, lg, dg, *, seq):
    nu = seq // N_BACK
    c4 = seq // 4
    left = lax.broadcasted_iota(jnp.int32, (N_BACK, LANES), 1) < HD_A

    def put_q(g, u, q):
        q = q * LOG2E
        qs[g, 2 * N_BACK * u:2 * N_BACK * u + N_BACK, :] = jnp.where(left, q, 0.0).astype(BF16)
        qs[g, 2 * N_BACK * u + N_BACK:2 * N_BACK * (u + 1), :] = jnp.where(left, 0.0, q).astype(BF16)

    @pl.when((pl.program_id(0) == 0) & (pl.program_id(1) == 0))
    def _():
        zpad = jnp.zeros((N_BACK, LANES), BF16)
        for buf in (k1, k4, v1, v4):
            buf[0:N_BACK, 0:LANES] = zpad
        for buf in (v1, v4, v16):
            buf[:, LANES:] = jnp.ones((buf.shape[0], LANES), BF16)

    k1[N_BACK:N_BACK + seq, :] = k_ref[...].astype(BF16)
    v1[N_BACK:N_BACK + seq, 0:LANES] = v_ref[...].astype(BF16)
    for r in range(4):
        by4 = pl.ds(r, c4, stride=4)
        tk[r * c4:(r + 1) * c4, :] = k_ref[by4, :]
        tv[r * c4:(r + 1) * c4, :] = v_ref[by4, :]
        tq[r * c4:(r + 1) * c4, :] = q2_ref[by4, :]
    k4[N_BACK:N_BACK + seq, :] = tk[...].astype(BF16)
    v4[N_BACK:N_BACK + seq, 0:LANES] = tv[...].astype(BF16)
    for u in range(nu):
        put_q(0, u, q0_ref[u * N_BACK:(u + 1) * N_BACK, :])
        put_q(1, u, q1_ref[pl.ds(u // 4 + 4 * N_BACK * (u % 4), N_BACK, stride=4), :])
        by16 = pl.ds((u % 4) * c4 + u // 4, N_BACK, stride=4)
        put_q(2, u, tq[by16, :])
        k16[u * N_BACK:(u + 1) * N_BACK, :] = tk[by16, :].astype(BF16)
        v16[u * N_BACK:(u + 1) * N_BACK, 0:LANES] = tv[by16, :].astype(BF16)

    shp = (2 * N_BACK, 2 * N_BACK)
    qi = lax.broadcasted_iota(jnp.int32, shp, 0) & (N_BACK - 1)
    kk = lax.broadcasted_iota(jnp.int32, shp, 1)
    step = N_BACK + qi - kk
    band = (step >= 0) & (step <= N_BACK)
    tri = band[:, N_BACK:]

    def units(it, carry, *, g, kbuf, vbuf, b_ref, class_blocks, staged):
        def scores(u):
            q = qs[g, pl.ds(pl.multiple_of(u * 2 * N_BACK, 2 * N_BACK), 2 * N_BACK), :]
            k0 = pl.multiple_of(u * N_BACK, N_BACK)
            if class_blocks > 1:
                krows = pl.ds(k0, 2 * N_BACK)
                first = (u & (class_blocks - 1)) == 0
                bias = b_ref[0, pl.ds(pl.multiple_of(jnp.where(first, 2 * N_BACK, 0), 2 * N_BACK), 2 * N_BACK), :]
                valid = band
            else:
                krows = pl.ds(k0, N_BACK)
                bias = b_ref[0, 0:2 * N_BACK, N_BACK:]
                valid = tri
            return k0, krows, jnp.where(valid, _dot_nt(q, kbuf[krows, :]) + bias, NEG)

        def weights(s):
            mx = jnp.max(s, axis=-1, keepdims=True)
            return mx, jnp.exp2(s - mx).astype(BF16)

        def finish(k0, krows, mx, p):
            pv = _dot(p, vbuf[krows, :])
            mxb = jnp.broadcast_to(mx, (2 * N_BACK, LANES))
            rows = pl.ds(k0, N_BACK)
            og[g, rows, :] = jnp.where(left, pv[0:N_BACK, 0:LANES], pv[N_BACK:, 0:LANES])
            dg[g, rows, :] = jnp.where(left, pv[0:N_BACK, LANES:], pv[N_BACK:, LANES:])
            lg[g, rows, :] = jnp.where(left, mxb[0:N_BACK], mxb[N_BACK:])

        us = [it * ATTN_UNITS + i for i in range(ATTN_UNITS)]
        if staged:
            sc = [scores(u) for u in us]
            wt = [weights(s) for _, _, s in sc]
            for (k0, krows, _), (mx, p) in zip(sc, wt):
                finish(k0, krows, mx, p)
        else:
            for u in us:
                k0, krows, s = scores(u)
                finish(k0, krows, *weights(s))
        return carry

    for g, (kbuf, vbuf, b_ref, class_blocks) in enumerate(((k1, v1, b0_ref, nu), (k4, v4, b1_ref, nu // 4),
                                                           (k16, v16, b2_ref, 1))):
        lax.fori_loop(0, nu // ATTN_UNITS, functools.partial(units, g=g, kbuf=kbuf, vbuf=vbuf, b_ref=b_ref,
                                                             class_blocks=class_blocks,
                                                             staged=class_blocks == 1), 0)

    for u in range(nu):
        by16 = pl.ds((u % 4) * c4 + u // 4, N_BACK, stride=4)
        tq[by16, :] = og[2, u * N_BACK:(u + 1) * N_BACK, :]
        tk[by16, :] = lg[2, u * N_BACK:(u + 1) * N_BACK, :]
        tv[by16, :] = dg[2, u * N_BACK:(u + 1) * N_BACK, :]
    for r in range(4):
        for jb in range(c4 // N_BACK):
            rows4 = slice(r * c4 + jb * N_BACK, r * c4 + (jb + 1) * N_BACK)
            tok = pl.ds(r + 4 * N_BACK * jb, N_BACK, stride=4)
            ms = [lg[0, tok, :], lg[1, rows4, :], tk[rows4, :]]
            mx = jnp.maximum(jnp.maximum(ms[0], ms[1]), ms[2])
            ws = [jnp.exp2(m - mx) for m in ms]
            num = ws[0] * og[0, tok, :] + ws[1] * og[1, rows4, :] + ws[2] * tq[rows4, :]
            den = ws[0] * dg[0, tok, :] + ws[1] * dg[1, rows4, :] + ws[2] * tv[rows4, :]
            o_ref[tok, :] = num / den


def _attn_prompt(qa, ka, va, biasmats, batch, seq):
    assert tuple(d for _, d in PATTERNS) == (1, 4, 16) and seq == 16 * N_BACK
    blk = (seq, LANES)
    qspec = lambda g: pl.BlockSpec(blk, lambda b, p: (b, g * N_PAIR + p))
    kspec = pl.BlockSpec(blk, lambda b, p: (b, p))
    bspec = pl.BlockSpec((1, 4 * N_BACK, 2 * N_BACK), lambda b, p: (p, 0, 0))
    before = np.arange(2 * N_BACK) < N_BACK
    pair_bias = []
    for bm in biasmats:
        t = (bm * LOG2E).reshape(N_PAIR, 2 * N_BACK, 2 * N_BACK)
        pair_bias.append(jnp.concatenate([t, jnp.where(before, NEG, t)], axis=1))
    act = lambda rows: pltpu.VMEM((rows, LANES), BF16)
    val = lambda rows: pltpu.VMEM((rows, 2 * LANES), BF16)
    return pl.pallas_call(
        functools.partial(_attn_prompt_kernel, seq=seq),
        grid=(batch, N_PAIR),
        in_specs=[qspec(0), qspec(1), qspec(2), kspec, kspec, bspec, bspec, bspec],
        out_specs=kspec,
        out_shape=jax.ShapeDtypeStruct((batch * seq, D_A), F32),
        scratch_shapes=[pltpu.VMEM((G_A, 2 * seq, LANES), BF16),
                        act(N_BACK + seq), val(N_BACK + seq), act(N_BACK + seq), val(N_BACK + seq),
                        act(seq), val(seq),
                        pltpu.VMEM((seq, LANES), F32), pltpu.VMEM((seq, LANES), F32), pltpu.VMEM((seq, LANES), F32),
                        pltpu.VMEM((G_A, seq, LANES), F32), pltpu.VMEM((G_A, seq, LANES), F32),
                        pltpu.VMEM((G_A, seq, LANES), F32)],
        compiler_params=_cparams(("arbitrary", "arbitrary")),
        name="attn_prompt",
    )(qa, qa, qa, ka, va, *pair_bias)


def _attn_sample_kernel(q_ref, kt_ref, vt_ref, knt_ref, vnt_ref, bc_ref, bn_ref, o_ref, *, wbuf, t_new):
    rows = G_A * t_new

    def allowed(shape, key0):
        rr = lax.broadcasted_iota(jnp.int32, shape, 0)
        col = lax.broadcasted_iota(jnp.int32, shape, 1)
        grp = rr >> (t_new.bit_length() - 1)
        dist = wbuf + (rr - grp * t_new) - (col + key0)
        dm1 = jnp.where(grp == 0, PATTERNS[0][1] - 1, jnp.where(grp == 1, PATTERNS[1][1] - 1, PATTERNS[2][1] - 1))
        win = jnp.where(grp == 0, PATTERNS[0][0], jnp.where(grp == 1, PATTERNS[1][0], PATTERNS[2][0]))
        return (dist >= 0) & ((dist & dm1) == 0) & (dist <= win)

    ok_c = allowed((rows, wbuf), 0)
    ok_n = allowed((rows, t_new), wbuf)
    for h in range(H_A):
        qh = q_ref[0, h].astype(BF16)
        sc = jnp.where(ok_c, _dot(qh, kt_ref[0, 0, h].astype(BF16)) + bc_ref[h], NEG)
        sn = jnp.where(ok_n, _dot(qh, knt_ref[0, h].astype(BF16)) + bn_ref[h], NEG)
        mx = jnp.maximum(jnp.max(sc, axis=-1, keepdims=True), jnp.max(sn, axis=-1, keepdims=True))
        mt = jnp.maximum(jnp.maximum(mx[0:t_new], mx[t_new:2 * t_new]), mx[2 * t_new:3 * t_new])
        mall = jnp.concatenate([mt, mt, mt], axis=0)
        pc = jnp.exp(sc - mall)
        pn = jnp.exp(sn - mall)
        den = jnp.sum(pc, axis=-1, keepdims=True) + jnp.sum(pn, axis=-1, keepdims=True)
        acc = (_dot_nt(pc.astype(BF16), vt_ref[0, 0, h].astype(BF16))
               + _dot_nt(pn.astype(BF16), vnt_ref[0, h].astype(BF16)))
        dt = den[0:t_new] + den[t_new:2 * t_new] + den[2 * t_new:3 * t_new]
        at = acc[0:t_new] + acc[t_new:2 * t_new] + acc[2 * t_new:3 * t_new]
        o_ref[0, h] = at / dt


def _attn_sample(q, kt_cache, vt_cache, layer, knt, vnt, bias_c, bias_n):
    batch, wbuf = kt_cache.shape[1], kt_cache.shape[4]
    t_new = knt.shape[3]
    rows = G_A * t_new
    b4 = lambda b: (b, 0, 0, 0)
    cache_spec = pl.BlockSpec((1, 1, H_A, HD_A, wbuf), lambda b: (layer, b, 0, 0, 0))
    new_spec = pl.BlockSpec((1, H_A, HD_A, t_new), b4)
    return pl.pallas_call(
        functools.partial(_attn_sample_kernel, wbuf=wbuf, t_new=t_new),
        grid=(batch,),
        in_specs=[pl.BlockSpec((1, H_A, rows, HD_A), b4), cache_spec, cache_spec, new_spec, new_spec,
                  _resident((H_A, rows, wbuf)), _resident((H_A, rows, t_new))],
        out_specs=pl.BlockSpec((1, H_A, t_new, HD_A), b4),
        out_shape=jax.ShapeDtypeStruct((batch, H_A, t_new, HD_A), F32),
        compiler_params=_cparams(("parallel",)),
        name="attn_sample",
    )(q, kt_cache, vt_cache, knt, vnt, bias_c, bias_n)


def _cache_update_kernel(kt_ref, vt_ref, kn_ref, vn_ref, ko_ref, vo_ref, *, wbuf, t_new):
    is_new = lax.broadcasted_iota(jnp.int32, (HD_A, LANES), 1) >= LANES - t_new
    for src, new, dst in ((kt_ref, kn_ref, ko_ref), (vt_ref, vn_ref, vo_ref)):
        for h in range(H_A):
            moved = pltpu.roll(src[0, 0, h], wbuf - t_new, 1)
            dst[0, 0, h, :, 0:wbuf - LANES] = moved[:, 0:wbuf - LANES]
            dst[0, 0, h, :, wbuf - LANES:wbuf] = jnp.where(is_new, new[0, 0, h], moved[:, wbuf - LANES:wbuf])


def _cache_update(kt_cache, vt_cache, kn_tail, vn_tail, t_new):
    depth, batch, _, _, wbuf = kt_cache.shape
    idx = lambda l, b: (l, b, 0, 0, 0)
    cache_spec = pl.BlockSpec((1, 1, H_A, HD_A, wbuf), idx)
    tail_spec = pl.BlockSpec((1, 1, H_A, HD_A, LANES), idx)
    shape = jax.ShapeDtypeStruct(kt_cache.shape, F32)
    return pl.pallas_call(
        functools.partial(_cache_update_kernel, wbuf=wbuf, t_new=t_new),
        grid=(depth, batch),
        in_specs=[cache_spec, cache_spec, tail_spec, tail_spec],
        out_specs=[cache_spec, cache_spec],
        out_shape=[shape, shape],
        compiler_params=_cparams(("parallel", "parallel")),
        name="cache_update",
    )(kt_cache, vt_cache, kn_tail, vn_tail)


S5_KB = 128
S5_NBLK = D_B // S5_KB
S5_SB = D_STATE // S5_NBLK
S5_TC = 128


def _s5_kernel(u_ref, x0r_ref, x0i_ref, ar_ref, ai_ref, bre_ref, bim_ref, cre_ref, cim_ref, d_ref, wglu_ref,
               yb_ref, xr_out, xi_out, ut, ybt, bur, bui, sr, si, *, nb, tc):
    @pl.when(pl.program_id(0) == 0)
    def _():
        sr[...] = x0r_ref[...]
        si[...] = x0i_ref[...]

    for b in range(nb):
        for c in range(D_B // LANES):
            ut[c, pl.ds(b, tc, stride=nb), :] = u_ref[b, :, c * LANES:(c + 1) * LANES]
    u = jnp.concatenate([ut[c] for c in range(D_B // LANES)], axis=1)
    ys = []
    for j in range(S5_NBLK):
        cs = slice(j * S5_SB, (j + 1) * S5_SB)
        uj = ut[j].astype(BF16)
        bur[:, cs] = _dot(uj, bre_ref[j])
        bui[:, cs] = _dot(uj, bim_ref[j])
        a_r, a_i = ar_ref[:, cs], ai_ref[:, cs]
        xr, xi = sr[:, cs], si[:, cs]
        for t in range(tc):
            rows = slice(t * nb, (t + 1) * nb)
            xr, xi = (a_r * xr - a_i * xi + bur[rows, cs], a_r * xi + a_i * xr + bui[rows, cs])
            bur[rows, cs] = xr
            bui[rows, cs] = xi
        sr[:, cs] = xr
        si[:, cs] = xi
        xr_out[:, cs] = xr
        xi_out[:, cs] = xi
        ys.append(_dot(bur[:, cs].astype(BF16), cre_ref[j]) - _dot(bui[:, cs].astype(BF16), cim_ref[j]))
    y = jnp.concatenate(ys, axis=-1) + d_ref[...] * u
    ab = _dot(jax.nn.gelu(y).astype(BF16), wglu_ref[...])
    for c in range(D_MODEL // LANES):
        cs = slice(c * LANES, (c + 1) * LANES)
        ybt[c] = ab[:, cs] * jax.nn.sigmoid(ab[:, D_MODEL + c * LANES:D_MODEL + (c + 1) * LANES])
    for b in range(nb):
        for c in range(D_MODEL // LANES):
            yb_ref[b, :, c * LANES:(c + 1) * LANES] = ybt[c, pl.ds(b, tc, stride=nb), :]


def _s5(u, x0r, x0i, prm, wglu, layer, tc):
    nb, seq, _ = u.shape
    r = tc * nb
    blk = lambda i: (0, i, 0)
    st = pl.BlockSpec((nb, D_STATE), lambda i: (0, 0))
    return pl.pallas_call(
        functools.partial(_s5_kernel, nb=nb, tc=tc),
        grid=(seq // tc,),
        in_specs=[pl.BlockSpec((nb, tc, D_B), blk), st, st,
                  _resident((1, D_STATE), layer), _resident((1, D_STATE), layer),
                  _resident((S5_NBLK, S5_KB, S5_SB), layer), _resident((S5_NBLK, S5_KB, S5_SB), layer),
                  _resident((S5_NBLK, S5_SB, S5_KB), layer), _resident((S5_NBLK, S5_SB, S5_KB), layer),
                  _resident((1, D_B), layer), _resident((D_B, 2 * D_MODEL), layer)],
        out_specs=[pl.BlockSpec((nb, tc, D_MODEL), blk), st, st],
        out_shape=[jax.ShapeDtypeStruct((nb, seq, D_MODEL), F32),
                   jax.ShapeDtypeStruct((nb, D_STATE), F32), jax.ShapeDtypeStruct((nb, D_STATE), F32)],
        scratch_shapes=[pltpu.VMEM((D_B // LANES, r, LANES), F32), pltpu.VMEM((D_MODEL // LANES, r, LANES), F32),
                        pltpu.VMEM((r, D_STATE), F32), pltpu.VMEM((r, D_STATE), F32),
                        pltpu.VMEM((nb, D_STATE), F32), pltpu.VMEM((nb, D_STATE), F32)],
        compiler_params=_cparams(("arbitrary",)),
        name="s5",
    )(u, x0r, x0i, prm["ar"], prm["ai"], prm["bre"], prm["bim"], prm["cre"], prm["cim"], prm["d"], wglu)


def _s5_params(log_dt, a_re, a_im, b_re, b_im, c_re, c_im, d_skip):
    depth = log_dt.shape[0]
    dt = jnp.exp(log_dt)[..., None]
    mag = jnp.exp(a_re * dt)
    ab_re = mag * jnp.cos(a_im * dt)
    ab_im = mag * jnp.sin(a_im * dt)
    den = a_re * a_re + a_im * a_im
    nr = ab_re - 1.0
    co_re = (nr * a_re + ab_im * a_im) / den
    co_im = (ab_im * a_re - nr * a_im) / den
    bb_re = co_re[..., None] * b_re - co_im[..., None] * b_im
    bb_im = co_re[..., None] * b_im + co_im[..., None] * b_re
    gpb = G_B // S5_NBLK
    eye = jnp.eye(gpb, dtype=F32)

    def pack_b(bb):
        t = bb.reshape(depth, S5_NBLK, gpb, N_B, P_B)
        m = jnp.einsum('ljgnp,gh->ljgphn', t, eye)
        return m.reshape(depth, S5_NBLK, gpb * P_B, gpb * N_B).astype(BF16)

    def pack_c(cc):
        t = cc.reshape(depth, S5_NBLK, gpb, P_B, N_B)
        m = jnp.einsum('ljgpn,gh->ljgnhp', t, eye)
        return m.reshape(depth, S5_NBLK, gpb * N_B, gpb * P_B).astype(BF16)

    return dict(ar=ab_re.reshape(depth, 1, D_STATE), ai=ab_im.reshape(depth, 1, D_STATE),
                bre=pack_b(bb_re), bim=pack_b(bb_im), cre=pack_c(c_re), cim=pack_c(c_im),
                d=d_skip.reshape(depth, 1, D_B))


def _gla_kernel(q_ref, k_ref, v_ref, r_ref, fc_ref, wfg_ref, bfg_ref, gn_ref, s0_ref, y_ref, sfin_ref, st,
                *, chunk, nchunk, bb):
    tb = pl.program_id(1)
    dk_all, dv_all = H_C * DK_C, H_C * DV_C

    def head_of(shape, dim, width):
        return lax.broadcasted_iota(jnp.int32, shape, dim) >> (width.bit_length() - 1)

    tok_all = H_C * chunk
    diag_k = head_of((tok_all, dk_all), 0, chunk) == head_of((tok_all, dk_all), 1, DK_C)
    diag_v = head_of((tok_all, dv_all), 0, chunk) == head_of((tok_all, dv_all), 1, DV_C)

    @pl.when(tb == 0)
    def _():
        for j in range(bb):
            for h in range(H_C):
                st[j * H_C + h] = s0_ref[j, h].T

    rows_t = chunk * nchunk
    sh = chunk.bit_length() - 1
    col_tok = lax.broadcasted_iota(jnp.int32, (chunk, tok_all), 1) & (chunk - 1)
    causal = col_tok <= lax.broadcasted_iota(jnp.int32, (chunk, tok_all), 0)
    gn = gn_ref[...]
    groups = [list(range(bb))] if rows_t < GLA_CHUNK else [[j] for j in range(bb)]
    for grp in groups:
        fc = fc_ref[grp[0]] if len(grp) == 1 else jnp.concatenate([fc_ref[j] for j in grp], axis=0)
        rows_g = rows_t * len(grp)
        la = jax.nn.log_sigmoid(_dot(fc.astype(BF16), wfg_ref[...]) + bfg_ref[...]) / GLA_TAU
        ri = lax.broadcasted_iota(jnp.int32, (rows_g, rows_g), 0)
        ci = lax.broadcasted_iota(jnp.int32, (rows_g, rows_g), 1)
        tri = jnp.where(((ri >> sh) == (ci >> sh)) & (ci <= ri), 1.0, 0.0).astype(BF16)
        la_hi = la.astype(BF16)
        la_lo = (la - la_hi.astype(F32)).astype(BF16)
        bcum = _dot(tri, la_hi) + _dot(tri, la_lo)
        for gi, j in enumerate(grp):
            for c in range(nchunk):
                rs = slice(c * chunk, (c + 1) * chunk)
                b = bcum[gi * rows_t + c * chunk:gi * rows_t + (c + 1) * chunk]
                bl = b[chunk - 1:chunk]
                qt = (q_ref[j, rs, :] * jnp.exp(b)).astype(BF16)
                kt = (k_ref[j, rs, :] * jnp.exp(-b)).astype(BF16)
                kd = (k_ref[j, rs, :] * jnp.exp(bl - b)).astype(BF16)
                dec = jnp.exp(bl)
                vb = v_ref[j, rs, :].astype(BF16)
                k_blk = jnp.where(diag_k, jnp.concatenate([kt] * H_C, axis=0), 0.0)
                v_blk = jnp.where(diag_v, jnp.concatenate([vb] * H_C, axis=0), 0.0)
                att = jnp.where(causal, _dot_nt(qt, k_blk), 0.0)
                o_intra = _dot(att.astype(BF16), v_blk)
                for h in range(H_C):
                    ks = slice(h * DK_C, (h + 1) * DK_C)
                    vs = slice(h * DV_C, (h + 1) * DV_C)
                    s_t = st[j * H_C + h]
                    o = o_intra[:, vs] + _dot_nt(qt[:, ks], s_t.astype(BF16))
                    st[j * H_C + h] = s_t * dec[:, ks] + _dot_tn(vb[:, vs], kd[:, ks])
                    y_ref[j, rs, vs] = _rms(o, gn) * jax.nn.silu(r_ref[j, rs, vs])

    @pl.when(tb == pl.num_programs(1) - 1)
    def _():
        for j in range(bb):
            for h in range(H_C):
                sfin_ref[j, h] = st[j * H_C + h].T


GLA_BLOCK_CHUNKS = 4
GLA_BATCH_ROWS = 4


def _gla(qc, kc, vc, rc, fc, wfg, bfg, gnorm, layer, s0, batch, seq):
    chunk = math.gcd(seq, GLA_CHUNK)
    nchunk = min(seq // chunk, GLA_BLOCK_CHUNKS)
    rows_t = chunk * nchunk
    nt = seq // rows_t
    bb = math.gcd(batch, max(1, GLA_CHUNK // rows_t) if nt == 1 else GLA_BATCH_ROWS)
    tok = lambda wd: pl.BlockSpec((bb, rows_t, wd), lambda b, t: (b, t, 0))
    st_spec = pl.BlockSpec((bb, H_C, DK_C, DV_C), lambda b, t: (b, 0, 0, 0))
    by_row = lambda a: a.reshape(batch, seq, a.shape[-1])
    y, s_fin = pl.pallas_call(
        functools.partial(_gla_kernel, chunk=chunk, nchunk=nchunk, bb=bb),
        grid=(batch // bb, nt),
        in_specs=[tok(H_C * DK_C), tok(H_C * DK_C), tok(H_C * DV_C), tok(H_C * DV_C), tok(FG_RANK),
                  _resident((FG_RANK, H_C * DK_C), layer), _resident((1, H_C * DK_C), layer),
                  _resident((1, DV_C), layer), st_spec],
        out_specs=[tok(H_C * DV_C), st_spec],
        out_shape=[jax.ShapeDtypeStruct((batch, seq, H_C * DV_C), F32),
                   jax.ShapeDtypeStruct((batch, H_C, DK_C, DV_C), F32)],
        scratch_shapes=[pltpu.VMEM((bb * H_C, DV_C, DK_C), F32)],
        compiler_params=_cparams(("parallel", "arbitrary")),
        name="gla",
    )(by_row(qc), by_row(kc), by_row(vc), by_row(rc), by_row(fc), wfg, bfg, gnorm, s0)
    return y.reshape(batch * seq, H_C * DV_C), s_fin


def _post_kernel(x_ref, oa_ref, gates_ref, yb_ref, yc_ref, woa_ref, woc_ref, wout_ref, gmlp_ref, wup_ref, wdn_ref,
                 gfin_ref, out_ref, *, final):
    y_a = _dot(oa_ref[...].astype(BF16), woa_ref[...])
    y_c = _dot(yc_ref[...].astype(BF16), woc_ref[...])
    gates = gates_ref[...]
    m = (jax.nn.sigmoid(gates[:, :D_MODEL]) * y_a + jax.nn.sigmoid(gates[:, D_MODEL:2 * D_MODEL]) * yb_ref[...]
         + jax.nn.sigmoid(gates[:, 2 * D_MODEL:]) * y_c)
    x1 = x_ref[...] + _dot(m.astype(BF16), wout_ref[...])
    hm = _rms(x1, gmlp_ref[...]).astype(BF16)
    up = _dot(hm, wup_ref[...])
    act = jnp.square(jnp.maximum(up, 0.0)).astype(BF16)
    x2 = x1 + _dot(act, wdn_ref[...])
    out_ref[...] = _rms(x2, gfin_ref[...]) if final else x2


def _post(x, oa, gates, yb, yc, lw, layer, gfin, final, tm):
    m = x.shape[0]
    row = lambda i: (i, 0)
    tok = lambda wd: pl.BlockSpec((tm, wd), row)
    return pl.pallas_call(
        functools.partial(_post_kernel, final=final),
        grid=(m // tm,),
        in_specs=[tok(D_MODEL), tok(D_A), tok(3 * D_MODEL), tok(D_MODEL), tok(H_C * DV_C),
                  _resident((D_A, D_MODEL), layer), _resident((H_C * DV_C, D_MODEL), layer),
                  _resident((D_MODEL, D_MODEL), layer), _resident((1, D_MODEL), layer),
                  _resident((D_MODEL, D_FF), layer), _resident((D_FF, D_MODEL), layer),
                  _resident((1, D_MODEL))],
        out_specs=tok(D_MODEL),
        out_shape=jax.ShapeDtypeStruct((m, D_MODEL), F32),
        compiler_params=_cparams(("parallel",)),
        name="post",
    )(x, oa, gates, yb, yc, lw["woa"], lw["woc"], lw["wout"], lw["gmlp"], lw["wup"], lw["wdn"], gfin)


def _t5_bucket(dist):
    exact = N_BUCKETS // 2
    d = np.maximum(dist, 1).astype(np.float32)
    large = exact + (np.log(d / exact) / np.log(W_MAX / exact) * (N_BUCKETS - exact)).astype(np.int32)
    large = np.minimum(large, N_BUCKETS - 1)
    return np.where(dist < exact, dist, large).astype(np.int32)


def _toeplitz(u, rows, cols, off):
    period = cols + off + 1
    w = jnp.pad(u, ((0, 0), (0, period - u.shape[1])))
    a = jnp.tile(w, (1, rows))[:, :rows * (period - 1)].reshape(u.shape[0], rows, period - 1)
    return a[:, :, off:off + cols]


def _prompt_bias(rel_bias, g, dil):
    x = np.arange(3 * N_BACK - 1)
    bk = _t5_bucket(np.clip(2 * N_BACK - 1 - x, 0, N_BACK) * dil)
    u = rel_bias[bk][:, g * H_A:(g + 1) * H_A].T
    return _toeplitz(u, N_BACK, 2 * N_BACK, N_BACK - 1)


def _sample_bias(rel_bias, wbuf, t_new):
    n = wbuf + t_new
    x = np.arange(n + t_new - 1)
    bk = _t5_bucket(np.clip(n - 1 - x, 0, W_MAX))
    tab = rel_bias[bk]
    per_g = [_toeplitz(tab[:, g * H_A:(g + 1) * H_A].T, t_new, n, t_new - 1) for g in range(G_A)]
    full = jnp.concatenate(per_g, axis=1)
    return full[:, :, :wbuf], full[:, :, wbuf:]


def _weights(w_in, w_o_a, w_glu, w_fg2, b_fg, gla_norm, w_o_c, w_out, norm_mix, norm_mlp, w_up, w_down):
    depth = w_in.shape[0]
    return dict(
        win=jnp.transpose(w_in, (0, 2, 1)).astype(BF16),
        gmix=norm_mix.reshape(depth, 1, D_MODEL),
        wglu=w_glu.astype(BF16),
        wfg=w_fg2.astype(BF16),
        bfg=b_fg.reshape(depth, 1, H_C * DK_C),
        gnorm=gla_norm.reshape(depth, 1, DV_C),
        woa=w_o_a.astype(BF16), woc=w_o_c.astype(BF16), wout=w_out.astype(BF16),
        gmlp=norm_mlp.reshape(depth, 1, D_MODEL),
        wup=w_up.astype(BF16), wdn=w_down.astype(BF16),
    )


def kernel(x_prompt, x_sample, cache_k_win, cache_v_win, state_ssm_re, state_ssm_im, state_gla, rel_bias, norm_mix, w_in, w_o_a, s5_log_dt, s5_a_re, s5_a_im, s5_b_re, s5_b_im, s5_c_re, s5_c_im, s5_d, w_glu, w_fg2, b_fg, gla_norm, w_o_c, w_out, norm_mlp, w_up, w_down, norm_final):
    bp, sp, _ = x_prompt.shape
    bs, ts, _ = x_sample.shape
    wbuf = cache_k_win.shape[2]
    mp, ms = bp * sp, bs * ts
    gfin = norm_final.reshape(1, D_MODEL)
    kt_cache = jnp.transpose(cache_k_win, (0, 1, 3, 4, 2))
    vt_cache = jnp.transpose(cache_v_win, (0, 1, 3, 4, 2))
    pbias = [_prompt_bias(rel_bias, g, dil) for g, (_, dil) in enumerate(PATTERNS)]
    sbias_c, sbias_n = _sample_bias(rel_bias, wbuf, ts)

    xp = x_prompt.reshape(mp, D_MODEL)
    xs = x_sample.reshape(ms, D_MODEL)
    outs = {k: [] for k in ("kp", "vp", "kn", "vn", "rp", "ip", "rs", "is", "gp", "gs")}
    lw = _weights(w_in, w_o_a, w_glu, w_fg2, b_fg, gla_norm, w_o_c, w_out, norm_mix, norm_mlp, w_up, w_down)
    s5p = _s5_params(s5_log_dt, s5_a_re, s5_a_im, s5_b_re, s5_b_im, s5_c_re, s5_c_im, s5_d)
    assert sp <= W_MAX
    for l in range(DEPTH):
        final = l == DEPTH - 1

        qa, ka, va, ub, qc, kc, vc, rc, fc, gates, kt, vt = _in_proj(xp, lw["gmix"], lw["win"], l, TM, (bp, sp))
        oa = _attn_prompt(qa, ka, va, pbias, bp, sp)
        zeros = jnp.zeros((bp, D_STATE), F32)
        yb, xr, xi = _s5(ub.reshape(bp, sp, D_B), zeros, zeros, s5p, lw["wglu"], l, S5_TC)
        yb = yb.reshape(mp, D_MODEL)
        yc, gst = _gla(qc, kc, vc, rc, fc, lw["wfg"], lw["bfg"], lw["gnorm"], l,
                       jnp.zeros((bp, H_C, DK_C, DV_C), F32), bp, sp)
        xp = _post(xp, oa, gates, yb, yc, lw, l, gfin, final, TM)
        outs["kp"].append(kt)
        outs["vp"].append(vt)
        outs["rp"].append(xr.reshape(bp, G_B, N_B))
        outs["ip"].append(xi.reshape(bp, G_B, N_B))
        outs["gp"].append(gst)

        qa, ka, va, ub, qc, kc, vc, rc, fc, gates = _in_proj(xs, lw["gmix"], lw["win"], l, TM)
        q4 =qa.reshape(bs, ts, G_A, H_A, HD_A).transpose(0, 3, 2, 1, 4).reshape(bs, H_A, G_A * ts, HD_A)
        knt = ka.reshape(bs, ts, H_A, HD_A).transpose(0, 2, 3, 1)
        vnt = va.reshape(bs, ts, H_A, HD_A).transpose(0, 2, 3, 1)
        o4 = _attn_sample(q4, kt_cache, vt_cache, l, knt, vnt, sbias_c, sbias_n)
        oa = o4.transpose(0, 2, 1, 3).reshape(ms, D_A)
        yb, xr, xi = _s5(ub.reshape(bs, ts, D_B), state_ssm_re[l].reshape(bs, D_STATE),
                         state_ssm_im[l].reshape(bs, D_STATE), s5p, lw["wglu"], l, ts)
        yb = yb.reshape(ms, D_MODEL)
        yc, gst = _gla(qc, kc, vc, rc, fc, lw["wfg"], lw["bfg"], lw["gnorm"], l, state_gla[l], bs, ts)
        xs = _post(xs, oa, gates, yb, yc, lw, l, gfin, final, TM)
        outs["kn"].append(knt)
        outs["vn"].append(vnt)
        outs["rs"].append(xr.reshape(bs, G_B, N_B))
        outs["is"].append(xi.reshape(bs, G_B, N_B))
        outs["gs"].append(gst)

    st = {k: jnp.stack(v) for k, v in outs.items()}
    tail = ((0, 0), (0, 0), (0, 0), (0, 0), (LANES - ts, 0))
    kt_new, vt_new = _cache_update(kt_cache, vt_cache, jnp.pad(st["kn"], tail), jnp.pad(st["vn"], tail), ts)
    to_rows = lambda t: jnp.transpose(t, (0, 1, 4, 2, 3))
    return (xp.reshape(bp, sp, D_MODEL), xs.reshape(bs, ts, D_MODEL),
            to_rows(st["kp"]), to_rows(st["vp"]), to_rows(kt_new), to_rows(vt_new),
            st["rp"], st["ip"], st["rs"], st["is"], st["gp"], st["gs"])
```
